```python
import math
import jax, jax.numpy as jnp
from jax import lax
import numpy as np

D_MODEL = 1024
BATCH = 4
SEQ = 4096
DEPTH = 2

GRID_W = 64
HA = 8
DA = 64
QB = 128
T5_BUCKETS = 32
T5_MAX_DIST = 128
HB = 16
DB = D_MODEL // HB
NA_ROWS = 8
NA_COLS = 16
D_FF = 2816
PLE_DIM = 256
EPS = 1e-6

kernel_name = "hybrid_diffattn_natten_macaron_encoder"


def rmsnorm(x, gain):
    x32 = x.astype(jnp.float32)
    y = x32 * lax.rsqrt(jnp.mean(x32 * x32, axis=-1, keepdims=True) + EPS)
    return (y * gain.astype(jnp.float32)).astype(x.dtype)


def swiglu(h, w_in, w_out):
    gu = h @ w_in
    g, u = jnp.split(gu, 2, axis=-1)
    return (jax.nn.silu(g) * u) @ w_out


def t5_bucket(rel):
    half = T5_BUCKETS // 2
    max_exact = half // 2
    ret = jnp.where(rel > 0, half, 0)
    n = jnp.abs(rel)
    nf = jnp.maximum(n, 1).astype(jnp.float32)
    large = max_exact + (jnp.log(nf / max_exact) / math.log(T5_MAX_DIST / max_exact)
                         * (half - max_exact)).astype(jnp.int32)
    large = jnp.minimum(large, half - 1)
    return ret + jnp.where(n < max_exact, n, large)


def diff_attention(h, w_qkv, w_o, q_gain, k_gain, lam_p, subln_gain, t5_table, layer_idx):
    B, S, _ = h.shape
    lambda_init = 0.8 - 0.6 * math.exp(-0.3 * layer_idx)
    qkv = h @ w_qkv
    q, k, v = jnp.split(qkv, 3, axis=-1)
    q = rmsnorm(q.reshape(B, S, HA, 2, DA), q_gain) * (DA ** -0.5)
    k = rmsnorm(k.reshape(B, S, HA, 2, DA), k_gain)
    v = v.reshape(B, S, HA, 2 * DA)
    lp = lam_p.astype(jnp.float32)
    lam = jnp.exp(jnp.sum(lp[0] * lp[1])) - jnp.exp(jnp.sum(lp[2] * lp[3])) + lambda_init
    nblk = S // QB
    qb = q.reshape(B, nblk, QB, HA, 2, DA).transpose(1, 0, 2, 3, 4, 5)
    kpos = jnp.arange(S, dtype=jnp.int32)
    table = t5_table.astype(jnp.float32)

    def block(args):
        qblk, start = args
        qpos = start + jnp.arange(QB, dtype=jnp.int32)
        bias = table[t5_bucket(kpos[None, :] - qpos[:, None])]
        bias = bias.transpose(2, 0, 1)
        s = jnp.einsum('bqhcd,bkhcd->bhcqk', qblk, k).astype(jnp.float32)
        s = s + bias[None, :, None]
        a = jax.nn.softmax(s, axis=-1)
        a = a[:, :, 0] - lam * a[:, :, 1]
        return jnp.einsum('bhqk,bkhe->bqhe', a.astype(v.dtype), v)

    starts = jnp.arange(nblk, dtype=jnp.int32) * QB
    o = lax.map(block, (qb, starts))
    o = o.transpose(1, 0, 2, 3, 4).reshape(B, S, HA, 2 * DA)
    o = rmsnorm(o, subln_gain) * (1.0 - lambda_init)
    return o.reshape(B, S, HA * 2 * DA) @ w_o


def neighbourhood_attention(h, w_qkv, w_o, q_gain, k_gain, rpb):
    B, S, _ = h.shape
    rows = S // GRID_W
    kr = min(NA_ROWS, rows)
    qkv = h @ w_qkv
    q, k, v = jnp.split(qkv, 3, axis=-1)
    q = rmsnorm(q.reshape(B, rows, GRID_W, HB, DB), q_gain) * (DB ** -0.5)
    k = rmsnorm(k.reshape(B, rows, GRID_W, HB, DB), k_gain)
    v = v.reshape(B, rows, GRID_W, HB, DB)
    cols = jnp.arange(GRID_W, dtype=jnp.int32)
    col_start = jnp.clip(cols - NA_COLS // 2, 0, GRID_W - NA_COLS)
    col_idx = col_start[:, None] + jnp.arange(NA_COLS, dtype=jnp.int32)[None, :]
    col_off = col_idx - cols[:, None] + (NA_COLS - 1)
    rpb32 = rpb.astype(jnp.float32)

    def row_step(r):
        rs = jnp.clip(r - kr // 2, 0, rows - kr)
        kb = lax.dynamic_slice_in_dim(k, rs, kr, axis=1)
        vb = lax.dynamic_slice_in_dim(v, rs, kr, axis=1)
        kg = kb[:, :, col_idx]
        vg = vb[:, :, col_idx]
        qr = lax.dynamic_index_in_dim(q, r, axis=1, keepdims=False)
        s = jnp.einsum('bqhd,brqjhd->bhqrj', qr, kg).astype(jnp.float32)
        row_off = rs + jnp.arange(kr, dtype=jnp.int32) - r + (NA_ROWS - 1)
        bias = rpb32[:, row_off[None, :, None], col_off[:, None, :]]
        s = s + bias[None]
        a = jax.nn.softmax(s.reshape(B, HB, GRID_W, kr * NA_COLS), axis=-1)
        a = a.reshape(B, HB, GRID_W, kr, NA_COLS).astype(v.dtype)
        return jnp.einsum('bhqrj,brqjhd->bqhd', a, vg)

    o = lax.map(row_step, jnp.arange(rows, dtype=jnp.int32))
    o = o.transpose(1, 0, 2, 3, 4).reshape(B, S, HB * DB)
    return o @ w_o


def setup_inputs(seed: int = 0) -> dict:
    key = jax.random.key(seed)
    ks = jax.random.split(key, 24)
    n_a = (DEPTH + 1) // 2
    n_b = DEPTH // 2
    f32 = jnp.float32
    nrm = lambda k, shape, s: (jax.random.normal(k, shape, f32) * s)
    return {
        "x": nrm(ks[0], (BATCH, SEQ, D_MODEL), 1.0),
        "p": nrm(ks[1], (DEPTH, BATCH, SEQ, PLE_DIM), 1.0),
        "norm_gains": 1.0 + nrm(ks[2], (DEPTH, 4, D_MODEL), 0.02),
        "w_ffn_in": nrm(ks[3], (DEPTH, 2, D_MODEL, 2 * D_FF), D_MODEL ** -0.5),
        "w_ffn_out": nrm(ks[4], (DEPTH, 2, D_FF, D_MODEL), D_FF ** -0.5),
        "t5_table": nrm(ks[5], (T5_BUCKETS, HA), 0.1),
        "a_w_qkv": nrm(ks[6], (n_a, D_MODEL, 3 * D_MODEL), D_MODEL ** -0.5),
        "a_w_o": nrm(ks[7], (n_a, HA * 2 * DA, D_MODEL), D_MODEL ** -0.5),
        "a_q_gain": 1.0 + nrm(ks[8], (n_a, DA), 0.02),
        "a_k_gain": 1.0 + nrm(ks[9], (n_a, DA), 0.02),
        "a_lambda": nrm(ks[10], (n_a, 4, DA), 0.1),
        "a_subln_gain": 1.0 + nrm(ks[11], (n_a, 2 * DA), 0.02),
        "b_w_qkv": nrm(ks[12], (n_b, D_MODEL, 3 * D_MODEL), D_MODEL ** -0.5),
        "b_w_o": nrm(ks[13], (n_b, HB * DB, D_MODEL), D_MODEL ** -0.5),
        "b_q_gain": 1.0 + nrm(ks[14], (n_b, DB), 0.02),
        "b_k_gain": 1.0 + nrm(ks[15], (n_b, DB), 0.02),
        "b_rpb": nrm(ks[16], (n_b, HB, 2 * NA_ROWS - 1, 2 * NA_COLS - 1), 0.1),
        "w_ple_gate": nrm(ks[17], (DEPTH, D_MODEL, D_MODEL), D_MODEL ** -0.5),
        "b_ple_gate": nrm(ks[18], (DEPTH, D_MODEL), 0.01),
        "w_ple_proj": nrm(ks[19], (DEPTH, PLE_DIM, D_MODEL), PLE_DIM ** -0.5),
    }


def reference(x, p, norm_gains, w_ffn_in, w_ffn_out, t5_table,
              a_w_qkv, a_w_o, a_q_gain, a_k_gain, a_lambda, a_subln_gain,
              b_w_qkv, b_w_o, b_q_gain, b_k_gain, b_rpb,
              w_ple_gate, b_ple_gate, w_ple_proj):
    h = x
    for i in range(DEPTH):
        h = h + 0.5 * swiglu(rmsnorm(h, norm_gains[i, 0]), w_ffn_in[i, 0], w_ffn_out[i, 0])
        hn = rmsnorm(h, norm_gains[i, 1])
        j = i // 2
        if i % 2 == 0:
            h = h + diff_attention(hn, a_w_qkv[j], a_w_o[j], a_q_gain[j], a_k_gain[j],
                                   a_lambda[j], a_subln_gain[j], t5_table, i)
        else:
            h = h + neighbourhood_attention(hn, b_w_qkv[j], b_w_o[j], b_q_gain[j],
                                            b_k_gain[j], b_rpb[j])
        h = h + 0.5 * swiglu(rmsnorm(h, norm_gains[i, 2]), w_ffn_in[i, 1], w_ffn_out[i, 1])
        gate = jax.nn.sigmoid(rmsnorm(h, norm_gains[i, 3]) @ w_ple_gate[i] + b_ple_gate[i])
        h = h + gate * (p[i] @ w_ple_proj[i])
    return h
```

```python
import functools
import math

import jax
import jax.numpy as jnp
from jax import lax
from jax.experimental import pallas as pl
from jax.experimental.pallas import tpu as pltpu

EPS = 1e-6
LOG2E = math.log2(math.e)
NEG_BIG = -1e30
LANES = 128
HEAD_DIM = 64
VMEM_LIMIT = 56 * 1024 * 1024

T5_BUCKETS = 32
T5_MAX_DIST = 128
GRID_W = 64
NA_ROWS = 8
NA_COLS = 16

TOKEN_TILE = 512
FF_CHUNK = 256
ATTN_TILE = 512

_F32 = jnp.float32
_BF16 = jnp.bfloat16


def _rms(x, gain):
    return x * lax.rsqrt(jnp.mean(x * x, axis=-1, keepdims=True) + EPS) * gain


def _dot(a, b):
    return jnp.dot(a, b, preferred_element_type=_F32)


def _dot_nt(a, b):
    return lax.dot_general(a, b, (((1,), (1,)), ((), ())), preferred_element_type=_F32)


def _resident(shape):
    nd = len(shape)
    return pl.BlockSpec(shape, lambda *_: (0,) * nd, pipeline_mode=pl.Buffered(1))


def _mlp_kernel(*refs, has_proj, has_ple, d_ff):
    it = iter(refs)
    h_ref = next(it)
    if has_proj:
        o_ref, wo_ref = next(it), next(it)
    g_ref, win_ref, wout_ref = next(it), next(it), next(it)
    if has_ple:
        g3_ref, wg_ref, bg_ref, p_ref, wp_ref = (next(it) for _ in range(5))
    out_ref = next(it)

    x = h_ref[...]
    if has_proj:
        x = x + _dot(o_ref[...], wo_ref[...])
    xn = _rms(x, g_ref[...]).astype(_BF16)
    acc = jnp.zeros_like(x)
    for c in range(d_ff // FF_CHUNK):
        lo = c * FF_CHUNK
        g = _dot(xn, win_ref[:, lo:lo + FF_CHUNK])
        u = _dot(xn, win_ref[:, d_ff + lo:d_ff + lo + FF_CHUNK])
        a = (g * jax.nn.sigmoid(g) * u).astype(_BF16)
        acc = acc + _dot(a, wout_ref[lo:lo + FF_CHUNK, :])
    y = x + 0.5 * acc
    if has_ple:
        yn = _rms(y, g3_ref[...]).astype(_BF16)
        gate = jax.nn.sigmoid(_dot(yn, wg_ref[...]) + bg_ref[...])
        y = y + gate * _dot(p_ref[...].astype(_BF16), wp_ref[...])
    out_ref[...] = y


def _mlp_call(h, gain, w_in, w_out, proj=None, ple=None):
    n, d = h.shape
    d_ff = w_out.shape[0]
    assert n % TOKEN_TILE == 0 and d_ff % FF_CHUNK == 0
    row = lambda width: pl.BlockSpec((TOKEN_TILE, width), lambda i: (i, 0))
    args, specs = [h], [row(d)]
    if proj is not None:
        o, w_o = proj
        args += [o, w_o]
        specs += [row(o.shape[1]), _resident(w_o.shape)]
    args += [gain.reshape(1, d), w_in, w_out]
    specs += [_resident((1, d)), _resident(w_in.shape), _resident(w_out.shape)]
    if ple is not None:
        g3, w_g, b_g, p, w_p = ple
        args += [g3.reshape(1, d), w_g, b_g.reshape(1, d), p, w_p]
        specs += [_resident((1, d)), _resident(w_g.shape), _resident((1, d)),
                  row(p.shape[1]), _resident(w_p.shape)]
    kern = functools.partial(_mlp_kernel, has_proj=proj is not None, has_ple=ple is not None, d_ff=d_ff)
    return pl.pallas_call(
        kern,
        grid=(n // TOKEN_TILE,),
        in_specs=specs,
        out_specs=row(d),
        out_shape=jax.ShapeDtypeStruct((n, d), _F32),
        compiler_params=pltpu.CompilerParams(dimension_semantics=("arbitrary",),
                                             vmem_limit_bytes=VMEM_LIMIT),
        name="mlp_proj_ple" if ple is not None else "mlp",
    )(*args)


def _head_rms_store(x, gain_ref, out_ref):
    low = lax.broadcasted_iota(jnp.int32, (1, LANES), 1) < HEAD_DIM
    for j in range(x.shape[1] // LANES):
        xb = x[:, j * LANES:(j + 1) * LANES]
        sq = xb * xb
        s_lo = jnp.sum(jnp.where(low, sq, 0.0), axis=-1, keepdims=True)
        s_hi = jnp.sum(jnp.where(low, 0.0, sq), axis=-1, keepdims=True)
        ms = jnp.where(low, s_lo, s_hi) * (1.0 / HEAD_DIM)
        yb = xb * lax.rsqrt(ms + EPS) * gain_ref[:, j * LANES:(j + 1) * LANES]
        out_ref[:, j * LANES:(j + 1) * LANES] = yb.astype(out_ref.dtype)


def _qkv_kernel(h_ref, g_ref, w_ref, qg_ref, kg_ref, q_ref, k_ref, v_ref):
    d = h_ref.shape[1]
    xn = _rms(h_ref[...], g_ref[...]).astype(_BF16)
    _head_rms_store(_dot(xn, w_ref[:, 0:d]), qg_ref, q_ref)
    _head_rms_store(_dot(xn, w_ref[:, d:2 * d]), kg_ref, k_ref)
    v_ref[...] = _dot(xn, w_ref[:, 2 * d:3 * d]).astype(v_ref.dtype)


def _qkv_call(h, gain, w_qkv, q_gain_lanes, k_gain_lanes):
    n, d = h.shape
    row = pl.BlockSpec((TOKEN_TILE, d), lambda i: (i, 0))
    out = jax.ShapeDtypeStruct((n, d), _BF16)
    return pl.pallas_call(
        _qkv_kernel,
        grid=(n // TOKEN_TILE,),
        in_specs=[row, _resident((1, d)), _resident(w_qkv.shape), _resident((1, d)), _resident((1, d))],
        out_specs=[row, row, row],
        out_shape=[out, out, out],
        compiler_params=pltpu.CompilerParams(dimension_semantics=("arbitrary",),
                                             vmem_limit_bytes=VMEM_LIMIT),
        name="qkv",
    )(h, gain.reshape(1, d), w_qkv, q_gain_lanes, k_gain_lanes)


def _diff_attn_kernel(q_ref, k_ref, v_ref, rb_ref, lam_ref, sg_ref, o_ref, bias_ref, *,
                      seq, tile, lambda_init):
    b, i = pl.program_id(1), pl.program_id(2)

    @pl.when((b == 0) & (i == 0))
    def _():
        for t in range(-2, 3):
            w = rb_ref[0, :, seq + (t - 1) * tile:seq + (t + 1) * tile]
            rolled = pltpu.roll(jnp.broadcast_to(w, (tile, 2 * tile)), 0, 1, stride=1, stride_axis=0)
            bias_ref[t + 2] = rolled[:, tile:]

    q = q_ref[...]
    low = lax.broadcasted_iota(jnp.int32, (1, LANES), 1) < HEAD_DIM
    zero = jnp.zeros_like(q)
    qs = jnp.concatenate([jnp.where(low, q, zero), jnp.where(low, zero, q)], axis=0)

    def body(j, carry):
        m, l, acc = carry
        start = pl.multiple_of(j * tile, tile)
        s = _dot_nt(qs, k_ref[pl.ds(start, tile), :])
        bias = bias_ref[jnp.clip(j - i, -2, 2) + 2]
        s = (s.reshape(2, tile, tile) + bias[None]).reshape(2 * tile, tile)
        m_new = jnp.maximum(m, jnp.max(s, axis=-1, keepdims=True))
        p = jnp.exp2(s - m_new)
        alpha = jnp.exp2(m - m_new)
        l = alpha * l + jnp.sum(p, axis=-1, keepdims=True)
        acc = alpha * acc + _dot(p.astype(_BF16), v_ref[pl.ds(start, tile), :])
        return m_new, l, acc

    init = (jnp.full((2 * tile, 1), NEG_BIG, _F32), jnp.zeros((2 * tile, 1), _F32),
            jnp.zeros((2 * tile, LANES), _F32))
    _, l, acc = lax.fori_loop(0, seq // tile, body, init)

    lp = lam_ref[...]
    lam = (jnp.exp(jnp.sum(lp[0:1] * lp[1:2], axis=-1, keepdims=True))
           - jnp.exp(jnp.sum(lp[2:3] * lp[3:4], axis=-1, keepdims=True)) + lambda_init)
    a = acc / l
    o = a[:tile] - lam * a[tile:]
    o_ref[...] = (_rms(o, sg_ref[...]) * (1.0 - lambda_init)).astype(o_ref.dtype)


def _diff_attn_call(q, k, v, rel_bias, lam_p, subln_gain, *, batch, seq, heads, lambda_init):
    n, d = q.shape
    tile = ATTN_TILE
    assert seq % tile == 0 and tile >= T5_MAX_DIST and seq >= 3 * tile and d == heads * LANES
    nq = seq // tile
    q_spec = pl.BlockSpec((tile, LANES), lambda h, b, i: (b * nq + i, h))
    kv_spec = pl.BlockSpec((seq, LANES), lambda h, b, i: (b, h))
    kern = functools.partial(_diff_attn_kernel, seq=seq, tile=tile, lambda_init=lambda_init)
    return pl.pallas_call(
        kern,
        grid=(heads, batch, nq),
        in_specs=[q_spec, kv_spec, kv_spec,
                  pl.BlockSpec((1, 1, 2 * seq), lambda h, b, i: (h, 0, 0)),
                  pl.BlockSpec(lam_p.shape, lambda h, b, i: (0, 0)),
                  pl.BlockSpec((1, LANES), lambda h, b, i: (0, 0))],
        out_specs=q_spec,
        out_shape=jax.ShapeDtypeStruct((n, d), _BF16),
        scratch_shapes=[pltpu.VMEM((5, tile, tile), _F32)],
        compiler_params=pltpu.CompilerParams(dimension_semantics=("arbitrary",) * 3,
                                             vmem_limit_bytes=VMEM_LIMIT),
        name="diff_attn",
    )(q, k, v, rel_bias, lam_p, subln_gain.reshape(1, LANES))


def _t5_bucket(rel):
    half = T5_BUCKETS // 2
    max_exact = half // 2
    ret = jnp.where(rel > 0, half, 0)
    n = jnp.abs(rel)
    nf = jnp.maximum(n, 1).astype(_F32)
    large = max_exact + (jnp.log(nf / max_exact) / math.log(T5_MAX_DIST / max_exact)
                         * (half - max_exact)).astype(jnp.int32)
    large = jnp.minimum(large, half - 1)
    return ret + jnp.where(n < max_exact, n, large)


def _natten_kernel(q_ref, k_ref, v_ref, bias_ref, o_ref, *, rows):
    w = GRID_W
    low = lax.broadcasted_iota(jnp.int32, (1, LANES), 1) < HEAD_DIM

    def body(r, carry):
        rs = jnp.clip(r - NA_ROWS // 2, 0, rows - NA_ROWS)
        q = q_ref[pl.ds(pl.multiple_of(r * w, w), w), :]
        zero = jnp.zeros_like(q)
        qs = jnp.concatenate([jnp.where(low, q, zero), jnp.where(low, zero, q)], axis=0)
        kstart = pl.multiple_of(rs * w, w)
        s = _dot_nt(qs, k_ref[pl.ds(kstart, NA_ROWS * w), :])
        s = s.reshape(2, w, NA_ROWS * w) + bias_ref[:, rs - r + NA_ROWS - 1]
        m = jnp.max(s, axis=-1, keepdims=True)
        p = jnp.exp2(s - m)
        l = jnp.sum(p, axis=-1, keepdims=True)
        pv = _dot(p.reshape(2 * w, NA_ROWS * w).astype(_BF16), v_ref[pl.ds(kstart, NA_ROWS * w), :])
        a = pv.reshape(2, w, LANES) / l
        o_ref[pl.ds(pl.multiple_of(r * w, w), w), :] = jnp.where(low, a[0], a[1]).astype(o_ref.dtype)
        return carry

    lax.fori_loop(0, rows, body, 0)


def _natten_call(q, k, v, bias, *, batch, seq):
    n, d = q.shape
    rows = seq // GRID_W
    spec = pl.BlockSpec((seq, LANES), lambda b, j: (b, j))
    return pl.pallas_call(
        functools.partial(_natten_kernel, rows=rows),
        grid=(batch, d // LANES),
        in_specs=[spec, spec, spec,
                  pl.BlockSpec((2,) + bias.shape[1:], lambda b, j: (j, 0, 0, 0))],
        out_specs=spec,
        out_shape=jax.ShapeDtypeStruct((n, d), _BF16),
        compiler_params=pltpu.CompilerParams(dimension_semantics=("arbitrary",) * 2,
                                             vmem_limit_bytes=VMEM_LIMIT),
        name="natten",
    )(q, k, v, bias)


def _natten_bias(rpb):
    cols = jnp.arange(GRID_W, dtype=jnp.int32)
    start = jnp.clip(cols - NA_COLS // 2, 0, GRID_W - NA_COLS)
    valid = (cols[None, :] >= start[:, None]) & (cols[None, :] < start[:, None] + NA_COLS)
    off = jnp.clip(cols[None, :] - cols[:, None] + (NA_COLS - 1), 0, 2 * NA_COLS - 2)
    per_row = jnp.where(valid[None, None], rpb.astype(_F32)[:, :, off] * LOG2E, NEG_BIG)
    strips = [jnp.concatenate([per_row[:, d0 + j] for j in range(NA_ROWS)], axis=-1)
              for d0 in range(NA_ROWS)]
    return jnp.stack(strips, axis=1)


def kernel(x, p, norm_gains, w_ffn_in, w_ffn_out, t5_table, a_w_qkv, a_w_o, a_q_gain, a_k_gain, a_lambda,
           a_subln_gain, b_w_qkv, b_w_o, b_q_gain, b_k_gain, b_rpb, w_ple_gate, b_ple_gate, w_ple_proj):
    batch, seq, d = x.shape
    depth = p.shape[0]
    n = batch * seq
    assert a_q_gain.shape[1] == HEAD_DIM and b_q_gain.shape[1] == HEAD_DIM and seq % GRID_W == 0
    bf = lambda w: w.astype(_BF16)
    lanes = lambda g, scale: jnp.tile(g.astype(_F32), d // g.shape[0]).reshape(1, d) * scale

    rel = jnp.arange(2 * seq, dtype=jnp.int32) - seq
    rel_bias = (t5_table.astype(_F32)[_t5_bucket(rel)] * LOG2E).T.reshape(-1, 1, 2 * seq)

    h = x.reshape(n, d)
    for i in range(depth):
        j = i // 2
        h = _mlp_call(h, norm_gains[i, 0], bf(w_ffn_in[i, 0]), bf(w_ffn_out[i, 0]))
        if i % 2 == 0:
            heads = d // LANES
            q, k, v = _qkv_call(h, norm_gains[i, 1], bf(a_w_qkv[j]),
                                lanes(a_q_gain[j], HEAD_DIM ** -0.5 * LOG2E), lanes(a_k_gain[j], 1.0))
            lambda_init = 0.8 - 0.6 * math.exp(-0.3 * i)
            o = _diff_attn_call(q, k, v, rel_bias, a_lambda[j], a_subln_gain[j], batch=batch, seq=seq,
                                heads=heads, lambda_init=lambda_init)
            w_o = a_w_o[j]
        else:
            q, k, v = _qkv_call(h, norm_gains[i, 1], bf(b_w_qkv[j]),
                                lanes(b_q_gain[j], HEAD_DIM ** -0.5 * LOG2E), lanes(b_k_gain[j], 1.0))
            o = _natten_call(q, k, v, _natten_bias(b_rpb[j]), batch=batch, seq=seq)
            w_o = b_w_o[j]
        h = _mlp_call(h, norm_gains[i, 2], bf(w_ffn_in[i, 1]), bf(w_ffn_out[i, 1]),
                      proj=(o, bf(w_o)),
                      ple=(norm_gains[i, 3], bf(w_ple_gate[i]), b_ple_gate[i], p[i].reshape(n, -1),
                           bf(w_ple_proj[i])))
    return h.reshape(batch, seq, d)
```

```python
import functools
import math

import jax
import jax.numpy as jnp
from jax import lax
from jax.experimental import pallas as pl
from jax.experimental.pallas import tpu as pltpu

EPS = 1e-6
LOG2E = math.log2(math.e)
NEG_BIG = -1e30
LANES = 128
HEAD_DIM = 64
VMEM_LIMIT = 56 * 1024 * 1024

T5_BUCKETS = 32
T5_MAX_DIST = 128
GRID_W = 64
NA_ROWS = 8
NA_COLS = 16

TOKEN_TILE = 512
FF_CHUNK = 256
ATTN_TILE = 512
QUERY_COLS = 256
NEAR_BLOCKS = 4
SUM_ROWS = 16

_F32 = jnp.float32
_BF16 = jnp.bfloat16


def _rms(x, gain):
    return x * lax.rsqrt(jnp.mean(x * x, axis=-1, keepdims=True) + EPS) * gain


def _dot(a, b):
    return jnp.dot(a, b, preferred_element_type=_F32)


def _dot_nt(a, b):
    return lax.dot_general(a, b, (((1,), (1,)), ((), ())), preferred_element_type=_F32)


def _resident(shape):
    nd = len(shape)
    return pl.BlockSpec(shape, lambda *_: (0,) * nd, pipeline_mode=pl.Buffered(1))


def _mlp_kernel(*refs, has_proj, has_ple, d_ff):
    it = iter(refs)
    h_ref = next(it)
    if has_proj:
        o_ref, wo_ref = next(it), next(it)
    g_ref, win_ref, wout_ref = next(it), next(it), next(it)
    if has_ple:
        g3_ref, wg_ref, bg_ref, p_ref, wp_ref = (next(it) for _ in range(5))
    out_ref = next(it)

    x = h_ref[...]
    if has_proj:
        x = x + _dot(o_ref[...], wo_ref[...])
    xn = _rms(x, g_ref[...]).astype(_BF16)
    acc = jnp.zeros_like(x)
    for c in range(d_ff // FF_CHUNK):
        lo = c * FF_CHUNK
        g = _dot(xn, win_ref[:, lo:lo + FF_CHUNK])
        u = _dot(xn, win_ref[:, d_ff + lo:d_ff + lo + FF_CHUNK])
        a = (g * jax.nn.sigmoid(g) * u).astype(_BF16)
        acc = acc + _dot(a, wout_ref[lo:lo + FF_CHUNK, :])
    y = x + 0.5 * acc
    if has_ple:
        yn = _rms(y, g3_ref[...]).astype(_BF16)
        gate = jax.nn.sigmoid(_dot(yn, wg_ref[...]) + bg_ref[...])
        y = y + gate * _dot(p_ref[...].astype(_BF16), wp_ref[...])
    out_ref[...] = y


def _mlp_call(h, gain, w_in, w_out, proj=None, ple=None):
    n, d = h.shape
    d_ff = w_out.shape[0]
    assert n % TOKEN_TILE == 0 and d_ff % FF_CHUNK == 0
    row = lambda width: pl.BlockSpec((TOKEN_TILE, width), lambda i: (i, 0))
    args, specs = [h], [row(d)]
    if proj is not None:
        o, w_o = proj
        args += [o, w_o]
        specs += [row(o.shape[1]), _resident(w_o.shape)]
    args += [gain.reshape(1, d), w_in, w_out]
    specs += [_resident((1, d)), _resident(w_in.shape), _resident(w_out.shape)]
    if ple is not None:
        g3, w_g, b_g, p, w_p = ple
        args += [g3.reshape(1, d), w_g, b_g.reshape(1, d), p, w_p]
        specs += [_resident((1, d)), _resident(w_g.shape), _resident((1, d)),
                  row(p.shape[1]), _resident(w_p.shape)]
    kern = functools.partial(_mlp_kernel, has_proj=proj is not None, has_ple=ple is not None, d_ff=d_ff)
    return pl.pallas_call(
        kern,
        grid=(n // TOKEN_TILE,),
        in_specs=specs,
        out_specs=row(d),
        out_shape=jax.ShapeDtypeStruct((n, d), _F32),
        compiler_params=pltpu.CompilerParams(dimension_semantics=("arbitrary",),
                                             vmem_limit_bytes=VMEM_LIMIT),
        name="mlp_proj_ple" if ple is not None else "mlp",
    )(*args)


def _head_rms_store(x, gain_ref, out_ref):
    low = lax.broadcasted_iota(jnp.int32, (1, LANES), 1) < HEAD_DIM
    for j in range(x.shape[1] // LANES):
        xb = x[:, j * LANES:(j + 1) * LANES]
        sq = xb * xb
        s_lo = jnp.sum(jnp.where(low, sq, 0.0), axis=-1, keepdims=True)
        s_hi = jnp.sum(jnp.where(low, 0.0, sq), axis=-1, keepdims=True)
        ms = jnp.where(low, s_lo, s_hi) * (1.0 / HEAD_DIM)
        yb = xb * lax.rsqrt(ms + EPS) * gain_ref[:, j * LANES:(j + 1) * LANES]
        out_ref[:, j * LANES:(j + 1) * LANES] = yb.astype(out_ref.dtype)


def _qkv_kernel(h_ref, g_ref, w_ref, qg_ref, kg_ref, q_ref, k_ref, v_ref):
    d = h_ref.shape[1]
    xn = _rms(h_ref[...], g_ref[...]).astype(_BF16)
    _head_rms_store(_dot(xn, w_ref[:, 0:d]), qg_ref, q_ref)
    _head_rms_store(_dot(xn, w_ref[:, d:2 * d]), kg_ref, k_ref)
    v_ref[...] = _dot(xn, w_ref[:, 2 * d:3 * d]).astype(v_ref.dtype)


def _qkv_call(h, gain, w_qkv, q_gain_lanes, k_gain_lanes):
    n, d = h.shape
    row = pl.BlockSpec((TOKEN_TILE, d), lambda i: (i, 0))
    out = jax.ShapeDtypeStruct((n, d), _BF16)
    return pl.pallas_call(
        _qkv_kernel,
        grid=(n // TOKEN_TILE,),
        in_specs=[row, _resident((1, d)), _resident(w_qkv.shape), _resident((1, d)), _resident((1, d))],
        out_specs=[row, row, row],
        out_shape=[out, out, out],
        compiler_params=pltpu.CompilerParams(dimension_semantics=("arbitrary",),
                                             vmem_limit_bytes=VMEM_LIMIT),
        name="qkv",
    )(h, gain.reshape(1, d), w_qkv, q_gain_lanes, k_gain_lanes)


def _diff_attn_kernel(q_ref, k_ref, vt_ref, rb_ref, lam_ref, sg_ref, o_ref,
                      bias_ref, qt_ref, m_ref, acc_ref, s0_ref, s1_ref, *, seq, tile, lambda_init):
    b, i = pl.program_id(1), pl.program_id(2)
    n_blocks = seq // tile

    @pl.when((b == 0) & (i == 0))
    def _():
        for t in range(-2, 3):
            w = rb_ref[0, :, seq - (t + 1) * tile:seq - (t - 1) * tile]
            rolled = pltpu.roll(jnp.broadcast_to(w, (tile, 2 * tile)), 0, 1, stride=1, stride_axis=0)
            bias_ref[t + 2] = rolled[:, tile:]

    qt = q_ref[...].astype(_F32).T
    low = lax.broadcasted_iota(jnp.int32, (LANES, 1), 0) < HEAD_DIM
    qt_ref[:, :tile] = jnp.where(low, qt, 0.0).astype(_BF16)
    qt_ref[:, tile:] = jnp.where(low, 0.0, qt).astype(_BF16)
    m_ref[...] = jnp.full(m_ref.shape, NEG_BIG, _F32)
    acc_ref[...] = jnp.zeros(acc_ref.shape, _F32)

    col_slices = [slice(c * QUERY_COLS, (c + 1) * QUERY_COLS) for c in range(2 * tile // QUERY_COLS)]

    def scores_into(s_ref, j, near):
        kb = k_ref[pl.ds(pl.multiple_of(j * tile, tile), tile), :]
        for cols in col_slices:
            s = _dot(kb, qt_ref[:, cols])
            if near:
                bcol = cols.start % tile
                s = s + bias_ref[jnp.clip(j - i, -2, 2) + 2, :, bcol:bcol + QUERY_COLS]
            s_ref[:, cols] = s

    def softmax_update(s_ref, j, const_bias):
        vt = vt_ref[0, 0, j]
        for cols in col_slices:
            m_old = m_ref[:, cols]
            col_max = jnp.max(s_ref[:, cols], axis=0, keepdims=True)
            if const_bias is not None:
                col_max = col_max + const_bias
            m_new = jnp.maximum(m_old, col_max)
            shift = m_new if const_bias is None else m_new - const_bias
            p = jnp.exp2(s_ref[:, cols] - shift).astype(_BF16)
            m_ref[:, cols] = m_new
            acc_ref[:, cols] = jnp.exp2(m_old - m_new) * acc_ref[:, cols] + _dot(vt, p)

    first_near = jnp.clip(i - 1, 0, n_blocks - NEAR_BLOCKS)
    bias_before = rb_ref[0, :, 2 * seq - 1:2 * seq]
    bias_after = rb_ref[0, :, 1:2]
    order = []
    for n in range(n_blocks):
        if n < NEAR_BLOCKS:
            order.append((first_near + n, True, None))
        else:
            f = n - NEAR_BLOCKS
            before = f < first_near
            order.append((jnp.where(before, f, f + NEAR_BLOCKS), False,
                          jnp.where(before, bias_before, bias_after)))
    bufs = (s0_ref, s1_ref)
    scores_into(bufs[0], order[0][0], order[0][1])
    for n, (j, _, const_bias) in enumerate(order):
        if n + 1 < n_blocks:
            scores_into(bufs[(n + 1) % 2], order[n + 1][0], order[n + 1][1])
        softmax_update(bufs[n % 2], j, const_bias)

    lp = lam_ref[...]
    lam = (jnp.exp(jnp.sum(lp[0:1] * lp[1:2], axis=-1, keepdims=True))
           - jnp.exp(jnp.sum(lp[2:3] * lp[3:4], axis=-1, keepdims=True)) + lambda_init)
    a = acc_ref[:LANES, :] / acc_ref[LANES:LANES + 1, :]
    ot = a[:, :tile] - lam * a[:, tile:]
    ms = jnp.mean(ot * ot, axis=0, keepdims=True)
    ot = ot * lax.rsqrt(ms + EPS) * (sg_ref[...] * (1.0 - lambda_init))
    o_ref[...] = ot.T.astype(o_ref.dtype)


def _diff_attn_call(q, k, v, rel_bias, lam_p, subln_gain, *, batch, seq, heads, lambda_init):
    n, d = q.shape
    tile = ATTN_TILE
    assert seq % tile == 0 and tile >= T5_MAX_DIST and seq >= 3 * tile and d == heads * LANES
    assert tile % QUERY_COLS == 0 and seq // tile >= NEAR_BLOCKS
    nq = seq // tile
    vt = v.reshape(batch, nq, tile, heads, LANES).transpose(0, 3, 1, 4, 2)
    vt = jnp.concatenate([vt, jnp.ones(vt.shape[:3] + (SUM_ROWS, tile), vt.dtype)], axis=3)
    q_spec = pl.BlockSpec((tile, LANES), lambda h, b, i: (b * nq + i, h))
    kern = functools.partial(_diff_attn_kernel, seq=seq, tile=tile, lambda_init=lambda_init)
    return pl.pallas_call(
        kern,
        grid=(heads, batch, nq),
        in_specs=[q_spec,
                  pl.BlockSpec((seq, LANES), lambda h, b, i: (b, h)),
                  pl.BlockSpec((1, 1, nq, LANES + SUM_ROWS, tile), lambda h, b, i: (b, h, 0, 0, 0)),
                  pl.BlockSpec((1, 1, 2 * seq), lambda h, b, i: (h, 0, 0)),
                  pl.BlockSpec(lam_p.shape, lambda h, b, i: (0, 0)),
                  pl.BlockSpec((LANES, 1), lambda h, b, i: (0, 0))],
        out_specs=q_spec,
        out_shape=jax.ShapeDtypeStruct((n, d), _BF16),
        scratch_shapes=[pltpu.VMEM((5, tile, tile), _F32),
                        pltpu.VMEM((LANES, 2 * tile), _BF16),
                        pltpu.VMEM((1, 2 * tile), _F32),
                        pltpu.VMEM((LANES + SUM_ROWS, 2 * tile), _F32),
                        pltpu.VMEM((tile, 2 * tile), _F32),
                        pltpu.VMEM((tile, 2 * tile), _F32)],
        compiler_params=pltpu.CompilerParams(dimension_semantics=("arbitrary",) * 3,
                                             vmem_limit_bytes=VMEM_LIMIT),
        name="diff_attn",
    )(q, k, vt, rel_bias, lam_p, subln_gain.reshape(LANES, 1))


def _t5_bucket(rel):
    half = T5_BUCKETS // 2
    max_exact = half // 2
    ret = jnp.where(rel > 0, half, 0)
    n = jnp.abs(rel)
    nf = jnp.maximum(n, 1).astype(_F32)
    large = max_exact + (jnp.log(nf / max_exact) / math.log(T5_MAX_DIST / max_exact)
                         * (half - max_exact)).astype(jnp.int32)
    large = jnp.minimum(large, half - 1)
    return ret + jnp.where(n < max_exact, n, large)


NA_SLAB_ROWS = NA_ROWS + 2
NA_EDGE_STEPS = 2
INVALID = 2 * NA_ROWS - 1
NA_UNROLL = 8


def _na_row_start(r, rows):
    return min(max(r - NA_ROWS // 2, 0), rows - NA_ROWS)


def _natten_kernel(q_ref, k_ref, vt_ref, tab_ref, o_ref, blk_ref, bias_ref, *, rows):
    w = GRID_W
    b = pl.program_id(1)
    lane = lax.broadcasted_iota(jnp.int32, (w, LANES), 1)
    low_lane = lane < w

    @pl.when(b == 0)
    def _():
        key_col = lax.broadcasted_iota(jnp.int32, (w, LANES), 0)
        col_start = jnp.clip((lane & (w - 1)) - NA_COLS // 2, 0, w - NA_COLS)
        in_window = (key_col >= col_start) & (key_col < col_start + NA_COLS)
        for e in range(2):
            for half in range(2):
                for dr in range(INVALID):
                    row = jnp.broadcast_to(tab_ref[e, half, dr:dr + 1, :], (w, LANES))
                    rolled = pltpu.roll(row, 0, 1, stride=1, stride_axis=0)
                    blk_ref[e, half, dr] = jnp.where(in_window, rolled, NEG_BIG)
                blk_ref[e, half, INVALID] = jnp.full((w, LANES), NEG_BIG, _F32)
        for variant, r0 in enumerate(_na_variant_rows(rows)):
            kstart = min(_na_row_start(r0, rows), rows - NA_SLAB_ROWS)
            for e in range(2):
                for dk in range(NA_SLAB_ROWS):
                    idx = []
                    for dq in range(2):
                        rs = _na_row_start(r0 + dq, rows)
                        inside = rs <= kstart + dk < rs + NA_ROWS
                        idx.append(kstart + dk - (r0 + dq) + NA_ROWS - 1 if inside else INVALID)
                    bias_ref[variant, dk * w:(dk + 1) * w, e * LANES:(e + 1) * LANES] = jnp.where(
                        low_lane, blk_ref[e, 0, idx[0]], blk_ref[e, 1, idx[1]])

    low_row = lax.broadcasted_iota(jnp.int32, (LANES, 1), 0) < HEAD_DIM
    n_steps = rows // 2

    def key_start(g):
        return jnp.minimum(jnp.clip(2 * g - NA_ROWS // 2, 0, rows - NA_ROWS), rows - NA_SLAB_ROWS)

    def query_rows(g):
        return pl.ds(pl.multiple_of(2 * g * w, 2 * w), 2 * w)

    def scores(g):
        variant = jnp.where(g < NA_EDGE_STEPS, g,
                            jnp.where(g >= n_steps - NA_EDGE_STEPS, g - (n_steps - 2 * NA_EDGE_STEPS - 1),
                                      NA_EDGE_STEPS))
        qt = q_ref[query_rows(g), :].astype(_F32).T
        qst = jnp.concatenate([jnp.where(low_row, qt, 0.0), jnp.where(low_row, 0.0, qt)],
                              axis=1).astype(_BF16)
        kw = k_ref[pl.ds(pl.multiple_of(key_start(g) * w, 2 * w), NA_SLAB_ROWS * w), :]
        return _dot(kw, qst) + bias_ref[variant]

    def probs(s):
        return jnp.exp2(s - jnp.max(s, axis=0, keepdims=True)).astype(_BF16)

    def output(g, p):
        vt = vt_ref[0, 0, pl.ds(key_start(g) // 2, NA_SLAB_ROWS // 2)]
        vt = jnp.concatenate([vt[t] for t in range(NA_SLAB_ROWS // 2)], axis=1)
        ot = _dot(vt, p)
        a = ot[:LANES] / ot[LANES:LANES + 1]
        o_ref[query_rows(g), :] = jnp.where(low_row, a[:, :LANES], a[:, LANES:]).T.astype(o_ref.dtype)

    def body(gg, carry):
        steps = [NA_UNROLL * gg + u for u in range(NA_UNROLL)]
        all_scores = [scores(g) for g in steps]
        all_probs = [probs(s) for s in all_scores]
        for g, p in zip(steps, all_probs):
            output(g, p)
        return carry

    lax.fori_loop(0, n_steps // NA_UNROLL, body, 0)


def _na_variant_rows(rows):
    lead = [2 * g for g in range(NA_EDGE_STEPS)]
    trail = [rows - 2 * NA_EDGE_STEPS + 2 * g for g in range(NA_EDGE_STEPS)]
    return lead + [2 * NA_EDGE_STEPS] + trail


def _natten_call(q, k, v, tab, *, batch, seq):
    n, d = q.shape
    rows = seq // GRID_W
    pairs = d // LANES
    assert GRID_W * 2 == LANES and NA_ROWS // 2 <= 2 * NA_EDGE_STEPS and rows % 4 == 0
    assert rows >= NA_SLAB_ROWS + 4 * NA_EDGE_STEPS
    vt = v.reshape(batch, rows // 2, LANES, pairs, LANES).transpose(0, 3, 1, 4, 2)
    vt = jnp.concatenate([vt, jnp.ones(vt.shape[:3] + (SUM_ROWS, LANES), vt.dtype)], axis=3)
    spec = pl.BlockSpec((seq, LANES), lambda j, b: (b, j))
    n_variants = 2 * NA_EDGE_STEPS + 1
    return pl.pallas_call(
        functools.partial(_natten_kernel, rows=rows),
        grid=(pairs, batch),
        in_specs=[spec, spec,
                  pl.BlockSpec((1, 1) + vt.shape[2:], lambda j, b: (b, j, 0, 0, 0)),
                  pl.BlockSpec((2,) + tab.shape[1:], lambda j, b: (j, 0, 0, 0))],
        out_specs=spec,
        out_shape=jax.ShapeDtypeStruct((n, d), _BF16),
        scratch_shapes=[pltpu.VMEM((2, 2, INVALID + 1, GRID_W, LANES), _F32),
                        pltpu.VMEM((n_variants, NA_SLAB_ROWS * GRID_W, 2 * LANES), _F32)],
        compiler_params=pltpu.CompilerParams(dimension_semantics=("arbitrary",) * 2,
                                             vmem_limit_bytes=VMEM_LIMIT),
        name="natten",
    )(q, k, vt, tab)


def _natten_table(rpb):
    rev = rpb.astype(_F32)[:, :, ::-1] * LOG2E
    left = GRID_W - NA_COLS + 1
    plain = jnp.pad(rev, ((0, 0), (0, 0), (left, LANES - left - rev.shape[-1])))
    return jnp.stack([jnp.roll(plain, GRID_W, axis=-1), plain], axis=1)


def kernel(x, p, norm_gains, w_ffn_in, w_ffn_out, t5_table, a_w_qkv, a_w_o, a_q_gain, a_k_gain, a_lambda,
           a_subln_gain, b_w_qkv, b_w_o, b_q_gain, b_k_gain, b_rpb, w_ple_gate, b_ple_gate, w_ple_proj):
    batch, seq, d = x.shape
    depth = p.shape[0]
    n = batch * seq
    assert a_q_gain.shape[1] == HEAD_DIM and b_q_gain.shape[1] == HEAD_DIM and seq % GRID_W == 0
    bf = lambda w: w.astype(_BF16)
    lanes = lambda g, scale: jnp.tile(g.astype(_F32), d // g.shape[0]).reshape(1, d) * scale

    rel = seq - jnp.arange(2 * seq, dtype=jnp.int32)
    rel_bias = (t5_table.astype(_F32)[_t5_bucket(rel)] * LOG2E).T.reshape(-1, 1, 2 * seq)

    h = x.reshape(n, d)
    for i in range(depth):
        j = i // 2
        h = _mlp_call(h, norm_gains[i, 0], bf(w_ffn_in[i, 0]), bf(w_ffn_out[i, 0]))
        if i % 2 == 0:
            heads = d // LANES
            q, k, v = _qkv_call(h, norm_gains[i, 1], bf(a_w_qkv[j]),
                                lanes(a_q_gain[j], HEAD_DIM ** -0.5 * LOG2E), lanes(a_k_gain[j], 1.0))
            lambda_init = 0.8 - 0.6 * math.exp(-0.3 * i)
            o = _diff_attn_call(q, k, v, rel_bias, a_lambda[j], a_subln_gain[j], batch=batch, seq=seq,
                                heads=heads, lambda_init=lambda_init)
            w_o = a_w_o[j]
        else:
            q, k, v = _qkv_call(h, norm_gains[i, 1], bf(b_w_qkv[j]),
                                lanes(b_q_gain[j], HEAD_DIM ** -0.5 * LOG2E), lanes(b_k_gain[j], 1.0))
            o = _natten_call(q, k, v, _natten_table(b_rpb[j]), batch=batch, seq=seq)
            w_o = b_w_o[j]
        h = _mlp_call(h, norm_gains[i, 2], bf(w_ffn_in[i, 1]), bf(w_ffn_out[i, 1]),
                      proj=(o, bf(w_o)),
                      ple=(norm_gains[i, 3], bf(w_ple_gate[i]), b_ple_gate[i], p[i].reshape(n, -1),
                           bf(w_ple_proj[i])))
    return h.reshape(batch, seq, d)
```

```python
import functools
import math

import jax
import jax.numpy as jnp
from jax import lax
from jax.experimental import pallas as pl
from jax.experimental.pallas import tpu as pltpu

EPS = 1e-6
LOG2E = math.log2(math.e)
NEG_BIG = -1e30
LANES = 128
HEAD_DIM = 64
VMEM_LIMIT = 56 * 1024 * 1024

T5_BUCKETS = 32
T5_MAX_DIST = 128
GRID_W = 64
NA_ROWS = 8
NA_COLS = 16

TOKEN_TILE = 512
FF_CHUNK = 256
ATTN_TILE = 512
QUERY_COLS = 256
NEAR_BLOCKS = 3
SUM_ROWS = 16

_F32 = jnp.float32
_BF16 = jnp.bfloat16


def _rms(x, gain):
    return x * lax.rsqrt(jnp.mean(x * x, axis=-1, keepdims=True) + EPS) * gain


def _dot(a, b):
    return jnp.dot(a, b, preferred_element_type=_F32)


def _dot_nt(a, b):
    return lax.dot_general(a, b, (((1,), (1,)), ((), ())), preferred_element_type=_F32)


def _resident(arr, lead=()):
    tail = arr.shape[len(lead):]
    index = tuple(lead) + (0,) * len(tail)
    return pl.BlockSpec((None,) * len(lead) + tail, lambda *_: index, pipeline_mode=pl.Buffered(1))


def _mlp_kernel(*refs, has_proj, has_ple, d_ff):
    it = iter(refs)
    h_ref = next(it)
    if has_proj:
        o_ref, wo_ref = next(it), next(it)
    g_ref, win_ref, wout_ref = next(it), next(it), next(it)
    if has_ple:
        g3_ref, wg_ref, bg_ref, p_ref, wp_ref = (next(it) for _ in range(5))
    out_ref = next(it)

    x = h_ref[...]
    if has_proj:
        x = x + _dot(o_ref[...], wo_ref[...])
    xn = _rms(x, g_ref[...]).astype(_BF16)
    acc = jnp.zeros_like(x)
    for c in range(d_ff // FF_CHUNK):
        lo = c * FF_CHUNK
        g = _dot(xn, win_ref[:, lo:lo + FF_CHUNK])
        u = _dot(xn, win_ref[:, d_ff + lo:d_ff + lo + FF_CHUNK])
        a = (g * jax.nn.sigmoid(g) * u).astype(_BF16)
        acc = acc + _dot(a, wout_ref[lo:lo + FF_CHUNK, :])
    y = x + 0.5 * acc
    if has_ple:
        yn = _rms(y, g3_ref[...]).astype(_BF16)
        gate = jax.nn.sigmoid(_dot(yn, wg_ref[...]) + bg_ref[...])
        y = y + gate * _dot(p_ref[...].astype(_BF16), wp_ref[...])
    out_ref[...] = y


def _mlp_call(h, gain, w_in, w_out, lead, proj=None, ple=None):
    n, d = h.shape
    d_ff = w_out.shape[-2]
    assert n % TOKEN_TILE == 0 and d_ff % FF_CHUNK == 0
    tiles = n // TOKEN_TILE
    row = lambda width, first=0: pl.BlockSpec((TOKEN_TILE, width), lambda i: (first + i, 0))
    vec = lambda v: v.reshape(1, d)
    args, specs = [h], [row(d)]
    if proj is not None:
        o, w_o, j = proj
        args += [o, w_o]
        specs += [row(o.shape[1]), _resident(w_o, (j,))]
    args += [vec(gain), w_in, w_out]
    specs += [_resident(vec(gain)), _resident(w_in, lead), _resident(w_out, lead)]
    if ple is not None:
        g3, w_g, b_g, p_rows, w_p, layer = ple
        args += [vec(g3), w_g, vec(b_g), p_rows, w_p]
        specs += [_resident(vec(g3)), _resident(w_g, (layer,)), _resident(vec(b_g)),
                  row(p_rows.shape[1], layer * tiles), _resident(w_p, (layer,))]
    kern = functools.partial(_mlp_kernel, has_proj=proj is not None, has_ple=ple is not None, d_ff=d_ff)
    return pl.pallas_call(
        kern,
        grid=(n // TOKEN_TILE,),
        in_specs=specs,
        out_specs=row(d),
        out_shape=jax.ShapeDtypeStruct((n, d), _F32),
        compiler_params=pltpu.CompilerParams(dimension_semantics=("arbitrary",),
                                             vmem_limit_bytes=VMEM_LIMIT),
        name="mlp_proj_ple" if ple is not None else "mlp",
    )(*args)


def _head_rms_store(x, gain_ref, out_ref):
    low = lax.broadcasted_iota(jnp.int32, (1, LANES), 1) < HEAD_DIM
    for j in range(x.shape[1] // LANES):
        xb = x[:, j * LANES:(j + 1) * LANES]
        sq = xb * xb
        s_lo = jnp.sum(jnp.where(low, sq, 0.0), axis=-1, keepdims=True)
        s_hi = jnp.sum(jnp.where(low, 0.0, sq), axis=-1, keepdims=True)
        ms = jnp.where(low, s_lo, s_hi) * (1.0 / HEAD_DIM)
        yb = xb * lax.rsqrt(ms + EPS) * gain_ref[:, j * LANES:(j + 1) * LANES]
        out_ref[:, j * LANES:(j + 1) * LANES] = yb.astype(out_ref.dtype)


def _qkv_kernel(h_ref, g_ref, w_ref, qg_ref, kg_ref, q_ref, k_ref, vt_ref):
    d = h_ref.shape[1]
    xn = _rms(h_ref[...], g_ref[...]).astype(_BF16)
    _head_rms_store(_dot(xn, w_ref[:, 0:d]), qg_ref, q_ref)
    _head_rms_store(_dot(xn, w_ref[:, d:2 * d]), kg_ref, k_ref)
    v = _dot(xn, w_ref[:, 2 * d:3 * d])
    groups, keys = vt_ref.shape[2], vt_ref.shape[4]
    ones = jnp.ones((SUM_ROWS, keys), vt_ref.dtype)
    for hb in range(d // LANES):
        vt = v[:, hb * LANES:(hb + 1) * LANES].T.astype(vt_ref.dtype)
        for t in range(groups):
            vt_ref[0, hb, t, :LANES, :] = vt[:, t * keys:(t + 1) * keys]
            vt_ref[0, hb, t, LANES:, :] = ones


def _qkv_call(h, gain, w_qkv, j, q_gain_lanes, k_gain_lanes, *, batch, keys):
    n, d = h.shape
    seq = n // batch
    assert seq % TOKEN_TILE == 0 and TOKEN_TILE % keys == 0
    tiles, groups = seq // TOKEN_TILE, TOKEN_TILE // keys
    row = pl.BlockSpec((TOKEN_TILE, d), lambda i: (i, 0))
    out = jax.ShapeDtypeStruct((n, d), _BF16)
    vt_shape = (batch, d // LANES, seq // keys, LANES + SUM_ROWS, keys)
    vt_spec = pl.BlockSpec((1, d // LANES, groups, LANES + SUM_ROWS, keys),
                           lambda i: (i // tiles, 0, i % tiles, 0, 0))
    return pl.pallas_call(
        _qkv_kernel,
        grid=(n // TOKEN_TILE,),
        in_specs=[row, _resident(gain.reshape(1, d)), _resident(w_qkv, (j,)), _resident(q_gain_lanes),
                  _resident(k_gain_lanes)],
        out_specs=[row, row, vt_spec],
        out_shape=[out, out, jax.ShapeDtypeStruct(vt_shape, _BF16)],
        compiler_params=pltpu.CompilerParams(dimension_semantics=("arbitrary",),
                                             vmem_limit_bytes=VMEM_LIMIT),
        name="qkv",
    )(h, gain.reshape(1, d), w_qkv, q_gain_lanes, k_gain_lanes)


def _diff_attn_kernel(q_ref, k_ref, vt_ref, rb_ref, lam_ref, sg_ref, o_ref,
                      bias_ref, qt_ref, m_ref, acc_ref, *, seq, tile, lambda_init):
    b, i = pl.program_id(1), pl.program_id(2)
    n_blocks = seq // tile

    @pl.when((b == 0) & (i == 0))
    def _():
        for t in range(-2, 3):
            w = rb_ref[0, :, seq - (t + 1) * tile:seq - (t - 1) * tile]
            rolled = pltpu.roll(jnp.broadcast_to(w, (tile, 2 * tile)), 0, 1, stride=1, stride_axis=0)
            bias_ref[t + 2] = rolled[:, tile:]

    qt = q_ref[...].astype(_F32).T
    low = lax.broadcasted_iota(jnp.int32, (LANES, 1), 0) < HEAD_DIM
    qt_ref[:, :tile] = jnp.where(low, qt, 0.0).astype(_BF16)
    qt_ref[:, tile:] = jnp.where(low, 0.0, qt).astype(_BF16)
    m_ref[...] = jnp.full(m_ref.shape, NEG_BIG, _F32)
    acc_ref[...] = jnp.zeros(acc_ref.shape, _F32)

    col_slices = [slice(c * QUERY_COLS, (c + 1) * QUERY_COLS) for c in range(2 * tile // QUERY_COLS)]

    def scores(j, near, cols):
        kb = k_ref[pl.ds(pl.multiple_of(j * tile, tile), tile), :]
        s = _dot(kb, qt_ref[:, cols])
        if near:
            bcol = cols.start % tile
            s = s + bias_ref[jnp.clip(j - i, -2, 2) + 2, :, bcol:bcol + QUERY_COLS]
        return s

    def softmax_update(s, j, const_bias, cols):
        m_old = m_ref[:, cols]
        col_max = jnp.max(s, axis=0, keepdims=True)
        if const_bias is not None:
            col_max = col_max + const_bias
        m_new = jnp.maximum(m_old, col_max)
        shift = m_new if const_bias is None else m_new - const_bias
        p = jnp.exp2(s - shift).astype(_BF16)
        m_ref[:, cols] = m_new
        acc_ref[:, cols] = jnp.exp2(m_old - m_new) * acc_ref[:, cols] + _dot(vt_ref[0, 0, j], p)

    first_near = jnp.clip(i - 1, 0, n_blocks - NEAR_BLOCKS)
    bias_before = rb_ref[0, :, 2 * seq - 1:2 * seq]
    bias_after = rb_ref[0, :, 1:2]
    order = []
    for n in range(n_blocks):
        if n < NEAR_BLOCKS:
            order.append((first_near + n, True, None))
        else:
            f = n - NEAR_BLOCKS
            before = f < first_near
            order.append((jnp.where(before, f, f + NEAR_BLOCKS), False,
                          jnp.where(before, bias_before, bias_after)))
    ahead = [scores(order[0][0], order[0][1], cols) for cols in col_slices]
    for n, (j, _, const_bias) in enumerate(order):
        for c, cols in enumerate(col_slices):
            current = ahead[c]
            if n + 1 < n_blocks:
                ahead[c] = scores(order[n + 1][0], order[n + 1][1], cols)
            softmax_update(current, j, const_bias, cols)

    lp = lam_ref[...]
    lam = (jnp.exp(jnp.sum(lp[0:1] * lp[1:2], axis=-1, keepdims=True))
           - jnp.exp(jnp.sum(lp[2:3] * lp[3:4], axis=-1, keepdims=True)) + lambda_init)
    a = acc_ref[:LANES, :] / acc_ref[LANES:LANES + 1, :]
    ot = a[:, :tile] - lam * a[:, tile:]
    ms = jnp.mean(ot * ot, axis=0, keepdims=True)
    ot = ot * lax.rsqrt(ms + EPS) * (sg_ref[...] * (1.0 - lambda_init))
    o_ref[...] = ot.T.astype(o_ref.dtype)


def _diff_attn_call(q, k, vt, rel_bias, lam_p, subln_gain, *, batch, seq, heads, lambda_init):
    n, d = q.shape
    tile = ATTN_TILE
    assert seq % tile == 0 and tile >= T5_MAX_DIST and seq >= 3 * tile and d == heads * LANES
    assert tile % QUERY_COLS == 0 and seq // tile >= NEAR_BLOCKS
    nq = seq // tile
    assert vt.shape == (batch, heads, nq, LANES + SUM_ROWS, tile)
    q_spec = pl.BlockSpec((tile, LANES), lambda h, b, i: (b * nq + i, h))
    kern = functools.partial(_diff_attn_kernel, seq=seq, tile=tile, lambda_init=lambda_init)
    return pl.pallas_call(
        kern,
        grid=(heads, batch, nq),
        in_specs=[q_spec,
                  pl.BlockSpec((seq, LANES), lambda h, b, i: (b, h)),
                  pl.BlockSpec((1, 1, nq, LANES + SUM_ROWS, tile), lambda h, b, i: (b, h, 0, 0, 0)),
                  pl.BlockSpec((1, 1, 2 * seq), lambda h, b, i: (h, 0, 0)),
                  pl.BlockSpec(lam_p.shape, lambda h, b, i: (0, 0)),
                  pl.BlockSpec((LANES, 1), lambda h, b, i: (0, 0))],
        out_specs=q_spec,
        out_shape=jax.ShapeDtypeStruct((n, d), _BF16),
        scratch_shapes=[pltpu.VMEM((5, tile, tile), _F32),
                        pltpu.VMEM((LANES, 2 * tile), _BF16),
                        pltpu.VMEM((1, 2 * tile), _F32),
                        pltpu.VMEM((LANES + SUM_ROWS, 2 * tile), _F32),
                        ],
        compiler_params=pltpu.CompilerParams(dimension_semantics=("arbitrary",) * 3,
                                             vmem_limit_bytes=VMEM_LIMIT),
        name="diff_attn",
    )(q, k, vt, rel_bias, lam_p, subln_gain.reshape(LANES, 1))


def _t5_bucket(rel):
    half = T5_BUCKETS // 2
    max_exact = half // 2
    ret = jnp.where(rel > 0, half, 0)
    n = jnp.abs(rel)
    nf = jnp.maximum(n, 1).astype(_F32)
    large = max_exact + (jnp.log(nf / max_exact) / math.log(T5_MAX_DIST / max_exact)
                         * (half - max_exact)).astype(jnp.int32)
    large = jnp.minimum(large, half - 1)
    return ret + jnp.where(n < max_exact, n, large)


NA_SLAB_ROWS = NA_ROWS + 2
NA_EDGE_STEPS = 2
INVALID = 2 * NA_ROWS - 1
NA_UNROLL = 16


def _na_row_start(r, rows):
    return min(max(r - NA_ROWS // 2, 0), rows - NA_ROWS)


def _natten_kernel(q_ref, k_ref, vt_ref, tab_ref, o_ref, blk_ref, bias_ref, *, rows):
    w = GRID_W
    b = pl.program_id(1)
    lane = lax.broadcasted_iota(jnp.int32, (w, LANES), 1)
    low_lane = lane < w

    @pl.when(b == 0)
    def _():
        key_col = lax.broadcasted_iota(jnp.int32, (w, LANES), 0)
        col_start = jnp.clip((lane & (w - 1)) - NA_COLS // 2, 0, w - NA_COLS)
        in_window = (key_col >= col_start) & (key_col < col_start + NA_COLS)
        for e in range(2):
            for half in range(2):
                for dr in range(INVALID):
                    row = jnp.broadcast_to(tab_ref[e, half, dr:dr + 1, :], (w, LANES))
                    rolled = pltpu.roll(row, 0, 1, stride=1, stride_axis=0)
                    blk_ref[e, half, dr] = jnp.where(in_window, rolled, NEG_BIG)
                blk_ref[e, half, INVALID] = jnp.full((w, LANES), NEG_BIG, _F32)
        for variant, r0 in enumerate(_na_variant_rows(rows)):
            kstart = min(_na_row_start(r0, rows), rows - NA_SLAB_ROWS)
            for e in range(2):
                for dk in range(NA_SLAB_ROWS):
                    idx = []
                    for dq in range(2):
                        rs = _na_row_start(r0 + dq, rows)
                        inside = rs <= kstart + dk < rs + NA_ROWS
                        idx.append(kstart + dk - (r0 + dq) + NA_ROWS - 1 if inside else INVALID)
                    bias_ref[variant, dk * w:(dk + 1) * w, e * LANES:(e + 1) * LANES] = jnp.where(
                        low_lane, blk_ref[e, 0, idx[0]], blk_ref[e, 1, idx[1]])

    low_row = lax.broadcasted_iota(jnp.int32, (LANES, 1), 0) < HEAD_DIM
    n_steps = rows // 2

    def key_start(g):
        return jnp.minimum(jnp.clip(2 * g - NA_ROWS // 2, 0, rows - NA_ROWS), rows - NA_SLAB_ROWS)

    def query_rows(g):
        return pl.ds(pl.multiple_of(2 * g * w, 2 * w), 2 * w)

    def scores(g):
        variant = jnp.where(g < NA_EDGE_STEPS, g,
                            jnp.where(g >= n_steps - NA_EDGE_STEPS, g - (n_steps - 2 * NA_EDGE_STEPS - 1),
                                      NA_EDGE_STEPS))
        qt = q_ref[query_rows(g), :].astype(_F32).T
        qst = jnp.concatenate([jnp.where(low_row, qt, 0.0), jnp.where(low_row, 0.0, qt)],
                              axis=1).astype(_BF16)
        kw = k_ref[pl.ds(pl.multiple_of(key_start(g) * w, 2 * w), NA_SLAB_ROWS * w), :]
        return _dot(kw, qst) + bias_ref[variant]

    def probs(s):
        return jnp.exp2(s - jnp.max(s, axis=0, keepdims=True)).astype(_BF16)

    def output(g, p):
        vt = vt_ref[0, 0, pl.ds(key_start(g) // 2, NA_SLAB_ROWS // 2)]
        vt = jnp.concatenate([vt[t] for t in range(NA_SLAB_ROWS // 2)], axis=1)
        ot = _dot(vt, p)
        a = ot[:LANES] / ot[LANES:LANES + 1]
        o_ref[query_rows(g), :] = jnp.where(low_row, a[:, :LANES], a[:, LANES:]).T.astype(o_ref.dtype)

    def body(gg, carry):
        steps = [NA_UNROLL * gg + u for u in range(NA_UNROLL)]
        all_scores = [scores(g) for g in steps]
        all_probs = [probs(s) for s in all_scores]
        for g, p in zip(steps, all_probs):
            output(g, p)
        return carry

    lax.fori_loop(0, n_steps // NA_UNROLL, body, 0)


def _na_variant_rows(rows):
    lead = [2 * g for g in range(NA_EDGE_STEPS)]
    trail = [rows - 2 * NA_EDGE_STEPS + 2 * g for g in range(NA_EDGE_STEPS)]
    return lead + [2 * NA_EDGE_STEPS] + trail


def _natten_call(q, k, vt, tab, *, batch, seq):
    n, d = q.shape
    rows = seq // GRID_W
    pairs = d // LANES
    assert GRID_W * 2 == LANES and NA_ROWS // 2 <= 2 * NA_EDGE_STEPS and rows % 4 == 0
    assert rows >= NA_SLAB_ROWS + 4 * NA_EDGE_STEPS
    assert vt.shape == (batch, pairs, rows // 2, LANES + SUM_ROWS, LANES)
    spec = pl.BlockSpec((seq, LANES), lambda j, b: (b, j))
    n_variants = 2 * NA_EDGE_STEPS + 1
    return pl.pallas_call(
        functools.partial(_natten_kernel, rows=rows),
        grid=(pairs, batch),
        in_specs=[spec, spec,
                  pl.BlockSpec((1, 1) + vt.shape[2:], lambda j, b: (b, j, 0, 0, 0)),
                  pl.BlockSpec((2,) + tab.shape[1:], lambda j, b: (j, 0, 0, 0))],
        out_specs=spec,
        out_shape=jax.ShapeDtypeStruct((n, d), _BF16),
        scratch_shapes=[pltpu.VMEM((2, 2, INVALID + 1, GRID_W, LANES), _F32),
                        pltpu.VMEM((n_variants, NA_SLAB_ROWS * GRID_W, 2 * LANES), _F32)],
        compiler_params=pltpu.CompilerParams(dimension_semantics=("arbitrary",) * 2,
                                             vmem_limit_bytes=VMEM_LIMIT),
        name="natten",
    )(q, k, vt, tab)


def _natten_table(rpb):
    rev = rpb.astype(_F32)[:, :, ::-1] * LOG2E
    left = GRID_W - NA_COLS + 1
    plain = jnp.pad(rev, ((0, 0), (0, 0), (left, LANES - left - rev.shape[-1])))
    return jnp.stack([jnp.roll(plain, GRID_W, axis=-1), plain], axis=1)


def kernel(x, p, norm_gains, w_ffn_in, w_ffn_out, t5_table, a_w_qkv, a_w_o, a_q_gain, a_k_gain, a_lambda,
           a_subln_gain, b_w_qkv, b_w_o, b_q_gain, b_k_gain, b_rpb, w_ple_gate, b_ple_gate, w_ple_proj):
    batch, seq, d = x.shape
    depth = p.shape[0]
    n = batch * seq
    assert a_q_gain.shape[1] == HEAD_DIM and b_q_gain.shape[1] == HEAD_DIM and seq % GRID_W == 0
    bf = lambda w: w.astype(_BF16)
    lanes = lambda g, scale: jnp.tile(g.astype(_F32), d // g.shape[0]).reshape(1, d) * scale

    rel = seq - jnp.arange(2 * seq, dtype=jnp.int32)
    in_bucket = _t5_bucket(rel)[None, :, None] == jnp.arange(T5_BUCKETS, dtype=jnp.int32)[None, None, :]
    rel_bias = jnp.sum(jnp.where(in_bucket, t5_table.astype(_F32).T[:, None, :] * LOG2E, 0.0), axis=-1)
    rel_bias = rel_bias.reshape(-1, 1, 2 * seq)

    w_in, w_out = bf(w_ffn_in), bf(w_ffn_out)
    w_gate, w_proj = bf(w_ple_gate), bf(w_ple_proj)
    qkv_a, qkv_b, wo_a, wo_b = bf(a_w_qkv), bf(b_w_qkv), bf(a_w_o), bf(b_w_o)
    p_rows = p.reshape(depth * n, -1)

    h = x.reshape(n, d)
    for i in range(depth):
        j = i // 2
        h = _mlp_call(h, norm_gains[i, 0], w_in, w_out, (i, 0))
        if i % 2 == 0:
            heads = d // LANES
            q, k, vt = _qkv_call(h, norm_gains[i, 1], qkv_a, j,
                                 lanes(a_q_gain[j], HEAD_DIM ** -0.5 * LOG2E), lanes(a_k_gain[j], 1.0),
                                 batch=batch, keys=ATTN_TILE)
            lambda_init = 0.8 - 0.6 * math.exp(-0.3 * i)
            o = _diff_attn_call(q, k, vt, rel_bias, a_lambda[j], a_subln_gain[j], batch=batch, seq=seq,
                                heads=heads, lambda_init=lambda_init)
            w_o = wo_a
        else:
            q, k, vt = _qkv_call(h, norm_gains[i, 1], qkv_b, j,
                                 lanes(b_q_gain[j], HEAD_DIM ** -0.5 * LOG2E), lanes(b_k_gain[j], 1.0),
                                 batch=batch, keys=2 * GRID_W)
            o = _natten_call(q, k, vt, _natten_table(b_rpb[j]), batch=batch, seq=seq)
            w_o = wo_b
        h = _mlp_call(h, norm_gains[i, 2], w_in, w_out, (i, 1), proj=(o, w_o, j),
                      ple=(norm_gains[i, 3], w_gate, b_ple_gate[i], p_rows, w_proj, i))
    return h.reshape(batch, seq, d)
```

```python
import functools
import math

import jax
import jax.numpy as jnp
from jax import lax
from jax.experimental import pallas as pl
from jax.experimental.pallas import tpu as pltpu

EPS = 1e-6
LOG2E = math.log2(math.e)
NEG_BIG = -1e30
LANES = 128
HEAD_DIM = 64
VMEM_LIMIT = 56 * 1024 * 1024

T5_BUCKETS = 32
T5_MAX_DIST = 128
GRID_W = 64
NA_ROWS = 8
NA_COLS = 16

TOKEN_TILE = 512
FF_CHUNK = 256
ATTN_TILE = 512
QUERY_COLS = 256
Q_PER_STEP = 2
NEAR_BLOCKS = 3
SAFE_EXP2_RANGE = 90.0
SUM_ROWS = 16

_F32 = jnp.float32
_BF16 = jnp.bfloat16


def _rms(x, gain):
    return x * lax.rsqrt(jnp.mean(x * x, axis=-1, keepdims=True) + EPS) * gain


def _dot(a, b):
    return jnp.dot(a, b, preferred_element_type=_F32)


def _dot_nt(a, b):
    return lax.dot_general(a, b, (((1,), (1,)), ((), ())), preferred_element_type=_F32)


def _resident(arr, lead=()):
    tail = arr.shape[len(lead):]
    index = tuple(lead) + (0,) * len(tail)
    return pl.BlockSpec((None,) * len(lead) + tail, lambda *_: index, pipeline_mode=pl.Buffered(1))


def _mlp_kernel(*refs, has_proj, has_ple, d_ff):
    it = iter(refs)
    h_ref = next(it)
    if has_proj:
        o_ref, wo_ref = next(it), next(it)
    g_ref, win_ref, wout_ref = next(it), next(it), next(it)
    if has_ple:
        g3_ref, wg_ref, bg_ref, p_ref, wp_ref = (next(it) for _ in range(5))
    out_ref = next(it)

    x = h_ref[...]
    if has_proj:
        x = x + _dot(o_ref[...], wo_ref[...])
    xn = _rms(x, g_ref[...]).astype(_BF16)
    acc = jnp.zeros_like(x)
    for c in range(d_ff // FF_CHUNK):
        lo = c * FF_CHUNK
        g = _dot(xn, win_ref[:, lo:lo + FF_CHUNK])
        u = _dot(xn, win_ref[:, d_ff + lo:d_ff + lo + FF_CHUNK])
        a = (g * jax.nn.sigmoid(g) * u).astype(_BF16)
        acc = acc + _dot(a, wout_ref[lo:lo + FF_CHUNK, :])
    y = x + 0.5 * acc
    if has_ple:
        yn = _rms(y, g3_ref[...]).astype(_BF16)
        gate = jax.nn.sigmoid(_dot(yn, wg_ref[...]) + bg_ref[...])
        y = y + gate * _dot(p_ref[...].astype(_BF16), wp_ref[...])
    out_ref[...] = y


def _mlp_call(h, gain, w_in, w_out, lead, proj=None, ple=None):
    n, d = h.shape
    d_ff = w_out.shape[-2]
    assert n % TOKEN_TILE == 0 and d_ff % FF_CHUNK == 0
    tiles = n // TOKEN_TILE
    row = lambda width, first=0: pl.BlockSpec((TOKEN_TILE, width), lambda i: (first + i, 0))
    vec = lambda v: v.reshape(1, d)
    args, specs = [h], [row(d)]
    if proj is not None:
        o, w_o, j = proj
        args += [o, w_o]
        specs += [row(o.shape[1]), _resident(w_o, (j,))]
    args += [vec(gain), w_in, w_out]
    specs += [_resident(vec(gain)), _resident(w_in, lead), _resident(w_out, lead)]
    if ple is not None:
        g3, w_g, b_g, p_rows, w_p, layer = ple
        args += [vec(g3), w_g, vec(b_g), p_rows, w_p]
        specs += [_resident(vec(g3)), _resident(w_g, (layer,)), _resident(vec(b_g)),
                  row(p_rows.shape[1], layer * tiles), _resident(w_p, (layer,))]
    kern = functools.partial(_mlp_kernel, has_proj=proj is not None, has_ple=ple is not None, d_ff=d_ff)
    return pl.pallas_call(
        kern,
        grid=(n // TOKEN_TILE,),
        in_specs=specs,
        out_specs=row(d),
        out_shape=jax.ShapeDtypeStruct((n, d), _F32),
        compiler_params=pltpu.CompilerParams(dimension_semantics=("arbitrary",),
                                             vmem_limit_bytes=VMEM_LIMIT),
        name="mlp_proj_ple" if ple is not None else "mlp",
    )(*args)


def _head_rms_store(x, gain_ref, out_ref):
    low = lax.broadcasted_iota(jnp.int32, (1, LANES), 1) < HEAD_DIM
    for j in range(x.shape[1] // LANES):
        xb = x[:, j * LANES:(j + 1) * LANES]
        sq = xb * xb
        s_lo = jnp.sum(jnp.where(low, sq, 0.0), axis=-1, keepdims=True)
        s_hi = jnp.sum(jnp.where(low, 0.0, sq), axis=-1, keepdims=True)
        ms = jnp.where(low, s_lo, s_hi) * (1.0 / HEAD_DIM)
        yb = xb * lax.rsqrt(ms + EPS) * gain_ref[:, j * LANES:(j + 1) * LANES]
        out_ref[:, j * LANES:(j + 1) * LANES] = yb.astype(out_ref.dtype)


def _qkv_kernel(h_ref, g_ref, w_ref, qg_ref, kg_ref, q_ref, k_ref, vt_ref):
    d = h_ref.shape[1]
    xn = _rms(h_ref[...], g_ref[...]).astype(_BF16)
    _head_rms_store(_dot(xn, w_ref[:, 0:d]), qg_ref, q_ref)
    _head_rms_store(_dot(xn, w_ref[:, d:2 * d]), kg_ref, k_ref)
    v = _dot(xn, w_ref[:, 2 * d:3 * d])
    groups, keys = vt_ref.shape[2], vt_ref.shape[4]
    ones = jnp.ones((SUM_ROWS, keys), vt_ref.dtype)
    for hb in range(d // LANES):
        vt = v[:, hb * LANES:(hb + 1) * LANES].T.astype(vt_ref.dtype)
        for t in range(groups):
            vt_ref[0, hb, t, :LANES, :] = vt[:, t * keys:(t + 1) * keys]
            vt_ref[0, hb, t, LANES:, :] = ones


def _qkv_call(h, gain, w_qkv, j, q_gain_lanes, k_gain_lanes, *, batch, keys):
    n, d = h.shape
    seq = n // batch
    assert seq % TOKEN_TILE == 0 and TOKEN_TILE % keys == 0
    tiles, groups = seq // TOKEN_TILE, TOKEN_TILE // keys
    row = pl.BlockSpec((TOKEN_TILE, d), lambda i: (i, 0))
    out = jax.ShapeDtypeStruct((n, d), _BF16)
    vt_shape = (batch, d // LANES, seq // keys, LANES + SUM_ROWS, keys)
    vt_spec = pl.BlockSpec((1, d // LANES, groups, LANES + SUM_ROWS, keys),
                           lambda i: (i // tiles, 0, i % tiles, 0, 0))
    return pl.pallas_call(
        _qkv_kernel,
        grid=(n // TOKEN_TILE,),
        in_specs=[row, _resident(gain.reshape(1, d)), _resident(w_qkv, (j,)), _resident(q_gain_lanes),
                  _resident(k_gain_lanes)],
        out_specs=[row, row, vt_spec],
        out_shape=[out, out, jax.ShapeDtypeStruct(vt_shape, _BF16)],
        compiler_params=pltpu.CompilerParams(dimension_semantics=("arbitrary",),
                                             vmem_limit_bytes=VMEM_LIMIT),
        name="qkv",
    )(h, gain.reshape(1, d), w_qkv, q_gain_lanes, k_gain_lanes)


def _diff_attn_kernel(q_ref, k_ref, vt_ref, rb_ref, lam_ref, sg_ref, o_ref,
                      bias_ref, qt_ref, acc_ref, *maybe_m_ref, seq, tile, lambda_init, stable):
    b, step = pl.program_id(1), pl.program_id(2)
    n_blocks = seq // tile

    @pl.when((b == 0) & (step == 0))
    def _():
        for t in range(-2, 3):
            w = rb_ref[0, :, seq - (t + 1) * tile:seq - (t - 1) * tile]
            rolled = pltpu.roll(jnp.broadcast_to(w, (tile, 2 * tile)), 0, 1, stride=1, stride_axis=0)
            bias_ref[t + 2] = rolled[:, tile:]

    low = lax.broadcasted_iota(jnp.int32, (LANES, 1), 0) < HEAD_DIM
    col_slices = [slice(c * QUERY_COLS, (c + 1) * QUERY_COLS) for c in range(2 * tile // QUERY_COLS)]
    bias_before = rb_ref[0, :, 2 * seq - 1:2 * seq]
    bias_after = rb_ref[0, :, 1:2]

    def prepare(u):
        qt = q_ref[u * tile:(u + 1) * tile, :].astype(_F32).T
        qt_ref[u, :, :tile] = jnp.where(low, qt, 0.0).astype(_BF16)
        qt_ref[u, :, tile:] = jnp.where(low, 0.0, qt).astype(_BF16)
        if stable:
            maybe_m_ref[0][u] = jnp.full(maybe_m_ref[0].shape[1:], NEG_BIG, _F32)
        acc_ref[u] = jnp.zeros(acc_ref.shape[1:], _F32)

    def block_order(i):
        first_near = jnp.clip(i - 1, 0, n_blocks - NEAR_BLOCKS)
        order = []
        for n in range(n_blocks):
            if n < NEAR_BLOCKS:
                order.append((first_near + n, True, None))
            else:
                f = n - NEAR_BLOCKS
                before = f < first_near
                order.append((jnp.where(before, f, f + NEAR_BLOCKS), False,
                              jnp.where(before, bias_before, bias_after)))
        return order

    def scores(u, i, j, near, cols):
        kb = k_ref[pl.ds(pl.multiple_of(j * tile, tile), tile), :]
        s = _dot(kb, qt_ref[u, :, cols])
        if near:
            bcol = cols.start % tile
            s = s + bias_ref[jnp.clip(j - i, -2, 2) + 2, :, bcol:bcol + QUERY_COLS]
        return s

    def softmax_update(u, s, j, const_bias, cols):
        if not stable:
            pv = _dot(vt_ref[0, 0, j], jnp.exp2(s).astype(_BF16))
            if const_bias is not None:
                pv = pv * jnp.exp2(const_bias)
            acc_ref[u, :, cols] += pv
            return
        m_ref = maybe_m_ref[0]
        m_old = m_ref[u, :, cols]
        col_max = jnp.max(s, axis=0, keepdims=True)
        if const_bias is not None:
            col_max = col_max + const_bias
        m_new = jnp.maximum(m_old, col_max)
        shift = m_new if const_bias is None else m_new - const_bias
        p = jnp.exp2(s - shift).astype(_BF16)
        m_ref[u, :, cols] = m_new
        acc_ref[u, :, cols] = jnp.exp2(m_old - m_new) * acc_ref[u, :, cols] + _dot(vt_ref[0, 0, j], p)

    def finish(u, lam):
        a = acc_ref[u, :LANES, :] / acc_ref[u, LANES:LANES + 1, :]
        ot = a[:, :tile] - lam * a[:, tile:]
        ms = jnp.mean(ot * ot, axis=0, keepdims=True)
        ot = ot * lax.rsqrt(ms + EPS) * (sg_ref[...] * (1.0 - lambda_init))
        o_ref[u * tile:(u + 1) * tile, :] = ot.T.astype(o_ref.dtype)

    streams = range(Q_PER_STEP)
    blocks = [Q_PER_STEP * step + u for u in streams]
    orders = [block_order(i) for i in blocks]
    for u in streams:
        prepare(u)
    ahead = [[scores(u, blocks[u], orders[u][0][0], orders[u][0][1], cols) for cols in col_slices]
             for u in streams]
    for n in range(n_blocks):
        for c, cols in enumerate(col_slices):
            for u in streams:
                j, _, const_bias = orders[u][n]
                current = ahead[u][c]
                if n + 1 < n_blocks:
                    ahead[u][c] = scores(u, blocks[u], orders[u][n + 1][0], orders[u][n + 1][1], cols)
                softmax_update(u, current, j, const_bias, cols)

    lp = lam_ref[...]
    lam = (jnp.exp(jnp.sum(lp[0:1] * lp[1:2], axis=-1, keepdims=True))
           - jnp.exp(jnp.sum(lp[2:3] * lp[3:4], axis=-1, keepdims=True)) + lambda_init)
    for u in streams:
        finish(u, lam)


def _diff_attn_call(q, k, vt, rel_bias, lam_p, subln_gain, *, batch, seq, heads, lambda_init, stable):
    n, d = q.shape
    tile = ATTN_TILE
    assert seq % tile == 0 and tile >= T5_MAX_DIST and seq >= 3 * tile and d == heads * LANES
    assert tile % QUERY_COLS == 0 and seq // tile >= NEAR_BLOCKS
    nq = seq // tile
    assert vt.shape == (batch, heads, nq, LANES + SUM_ROWS, tile) and nq % Q_PER_STEP == 0
    steps = nq // Q_PER_STEP
    q_spec = pl.BlockSpec((Q_PER_STEP * tile, LANES), lambda h, b, i: (b * steps + i, h))
    kern = functools.partial(_diff_attn_kernel, seq=seq, tile=tile, lambda_init=lambda_init, stable=stable)
    running_max = [pltpu.VMEM((Q_PER_STEP, 1, 2 * tile), _F32)] if stable else []
    return pl.pallas_call(
        kern,
        grid=(heads, batch, steps),
        in_specs=[q_spec,
                  pl.BlockSpec((seq, LANES), lambda h, b, i: (b, h)),
                  pl.BlockSpec((1, 1, nq, LANES + SUM_ROWS, tile), lambda h, b, i: (b, h, 0, 0, 0)),
                  pl.BlockSpec((1, 1, 2 * seq), lambda h, b, i: (h, 0, 0)),
                  pl.BlockSpec(lam_p.shape, lambda h, b, i: (0, 0)),
                  pl.BlockSpec((LANES, 1), lambda h, b, i: (0, 0))],
        out_specs=q_spec,
        out_shape=jax.ShapeDtypeStruct((n, d), _BF16),
        scratch_shapes=[pltpu.VMEM((5, tile, tile), _F32),
                        pltpu.VMEM((Q_PER_STEP, LANES, 2 * tile), _BF16),
                        pltpu.VMEM((Q_PER_STEP, LANES + SUM_ROWS, 2 * tile), _F32),
                        ] + running_max,
        compiler_params=pltpu.CompilerParams(dimension_semantics=("arbitrary",) * 3,
                                             vmem_limit_bytes=VMEM_LIMIT),
        name="diff_attn" if stable else "diff_attn_bounded",
    )(q, k, vt, rel_bias, lam_p, subln_gain.reshape(LANES, 1))


def _t5_bucket(rel):
    half = T5_BUCKETS // 2
    max_exact = half // 2
    ret = jnp.where(rel > 0, half, 0)
    n = jnp.abs(rel)
    nf = jnp.maximum(n, 1).astype(_F32)
    large = max_exact + (jnp.log(nf / max_exact) / math.log(T5_MAX_DIST / max_exact)
                         * (half - max_exact)).astype(jnp.int32)
    large = jnp.minimum(large, half - 1)
    return ret + jnp.where(n < max_exact, n, large)


NA_SLAB_ROWS = NA_ROWS + 2
NA_EDGE_STEPS = 2
INVALID = 2 * NA_ROWS - 1
NA_UNROLL = 16


def _na_row_start(r, rows):
    return min(max(r - NA_ROWS // 2, 0), rows - NA_ROWS)


def _natten_kernel(q_ref, k_ref, vt_ref, tab_ref, o_ref, blk_ref, bias_ref, *, rows, stable):
    w = GRID_W
    b = pl.program_id(1)
    lane = lax.broadcasted_iota(jnp.int32, (w, LANES), 1)
    low_lane = lane < w

    @pl.when(b == 0)
    def _():
        key_col = lax.broadcasted_iota(jnp.int32, (w, LANES), 0)
        col_start = jnp.clip((lane & (w - 1)) - NA_COLS // 2, 0, w - NA_COLS)
        in_window = (key_col >= col_start) & (key_col < col_start + NA_COLS)
        for e in range(2):
            for half in range(2):
                for dr in range(INVALID):
                    row = jnp.broadcast_to(tab_ref[e, half, dr:dr + 1, :], (w, LANES))
                    rolled = pltpu.roll(row, 0, 1, stride=1, stride_axis=0)
                    blk_ref[e, half, dr] = jnp.where(in_window, rolled, NEG_BIG)
                blk_ref[e, half, INVALID] = jnp.full((w, LANES), NEG_BIG, _F32)
        for variant, r0 in enumerate(_na_variant_rows(rows)):
            kstart = min(_na_row_start(r0, rows), rows - NA_SLAB_ROWS)
            for e in range(2):
                for dk in range(NA_SLAB_ROWS):
                    idx = []
                    for dq in range(2):
                        rs = _na_row_start(r0 + dq, rows)
                        inside = rs <= kstart + dk < rs + NA_ROWS
                        idx.append(kstart + dk - (r0 + dq) + NA_ROWS - 1 if inside else INVALID)
                    bias_ref[variant, dk * w:(dk + 1) * w, e * LANES:(e + 1) * LANES] = jnp.where(
                        low_lane, blk_ref[e, 0, idx[0]], blk_ref[e, 1, idx[1]])

    low_row = lax.broadcasted_iota(jnp.int32, (LANES, 1), 0) < HEAD_DIM
    n_steps = rows // 2

    def key_start(g):
        return jnp.minimum(jnp.clip(2 * g - NA_ROWS // 2, 0, rows - NA_ROWS), rows - NA_SLAB_ROWS)

    def query_rows(g):
        return pl.ds(pl.multiple_of(2 * g * w, 2 * w), 2 * w)

    def scores(g):
        variant = jnp.where(g < NA_EDGE_STEPS, g,
                            jnp.where(g >= n_steps - NA_EDGE_STEPS, g - (n_steps - 2 * NA_EDGE_STEPS - 1),
                                      NA_EDGE_STEPS))
        qt = q_ref[query_rows(g), :].astype(_F32).T
        qst = jnp.concatenate([jnp.where(low_row, qt, 0.0), jnp.where(low_row, 0.0, qt)],
                              axis=1).astype(_BF16)
        kw = k_ref[pl.ds(pl.multiple_of(key_start(g) * w, 2 * w), NA_SLAB_ROWS * w), :]
        return _dot(kw, qst) + bias_ref[variant]

    def probs(s):
        if stable:
            s = s - jnp.max(s, axis=0, keepdims=True)
        return jnp.exp2(s).astype(_BF16)

    def output(g, p):
        vt = vt_ref[0, 0, pl.ds(key_start(g) // 2, NA_SLAB_ROWS // 2)]
        vt = jnp.concatenate([vt[t] for t in range(NA_SLAB_ROWS // 2)], axis=1)
        ot = _dot(vt, p)
        a = ot[:LANES] / ot[LANES:LANES + 1]
        o_ref[query_rows(g), :] = jnp.where(low_row, a[:, :LANES], a[:, LANES:]).T.astype(o_ref.dtype)

    def body(gg, carry):
        steps = [NA_UNROLL * gg + u for u in range(NA_UNROLL)]
        all_scores = [scores(g) for g in steps]
        all_probs = [probs(s) for s in all_scores]
        for g, p in zip(steps, all_probs):
            output(g, p)
        return carry

    lax.fori_loop(0, n_steps // NA_UNROLL, body, 0)


def _na_variant_rows(rows):
    lead = [2 * g for g in range(NA_EDGE_STEPS)]
    trail = [rows - 2 * NA_EDGE_STEPS + 2 * g for g in range(NA_EDGE_STEPS)]
    return lead + [2 * NA_EDGE_STEPS] + trail


def _natten_call(q, k, vt, tab, *, batch, seq, stable):
    n, d = q.shape
    rows = seq // GRID_W
    pairs = d // LANES
    assert GRID_W * 2 == LANES and NA_ROWS // 2 <= 2 * NA_EDGE_STEPS and rows % 4 == 0
    assert rows >= NA_SLAB_ROWS + 4 * NA_EDGE_STEPS
    assert vt.shape == (batch, pairs, rows // 2, LANES + SUM_ROWS, LANES)
    spec = pl.BlockSpec((seq, LANES), lambda j, b: (b, j))
    n_variants = 2 * NA_EDGE_STEPS + 1
    return pl.pallas_call(
        functools.partial(_natten_kernel, rows=rows, stable=stable),
        grid=(pairs, batch),
        in_specs=[spec, spec,
                  pl.BlockSpec((1, 1) + vt.shape[2:], lambda j, b: (b, j, 0, 0, 0)),
                  pl.BlockSpec((2,) + tab.shape[1:], lambda j, b: (j, 0, 0, 0))],
        out_specs=spec,
        out_shape=jax.ShapeDtypeStruct((n, d), _BF16),
        scratch_shapes=[pltpu.VMEM((2, 2, INVALID + 1, GRID_W, LANES), _F32),
                        pltpu.VMEM((n_variants, NA_SLAB_ROWS * GRID_W, 2 * LANES), _F32)],
        compiler_params=pltpu.CompilerParams(dimension_semantics=("arbitrary",) * 2,
                                             vmem_limit_bytes=VMEM_LIMIT),
        name="natten" if stable else "natten_bounded",
    )(q, k, vt, tab)


def _natten_table(rpb):
    rev = rpb.astype(_F32)[:, :, ::-1] * LOG2E
    left = GRID_W - NA_COLS + 1
    plain = jnp.pad(rev, ((0, 0), (0, 0), (left, LANES - left - rev.shape[-1])))
    return jnp.stack([jnp.roll(plain, GRID_W, axis=-1), plain], axis=1)


def _scores_bounded(q_gain, k_gain, bias_table):
    f32 = lambda a: jnp.max(jnp.abs(a.astype(_F32)))
    bound = LOG2E * (1.02 * HEAD_DIM ** 0.5 * f32(q_gain) * f32(k_gain) + f32(bias_table))
    return bound <= SAFE_EXP2_RANGE


def kernel(x, p, norm_gains, w_ffn_in, w_ffn_out, t5_table, a_w_qkv, a_w_o, a_q_gain, a_k_gain, a_lambda,
           a_subln_gain, b_w_qkv, b_w_o, b_q_gain, b_k_gain, b_rpb, w_ple_gate, b_ple_gate, w_ple_proj):
    batch, seq, d = x.shape
    depth = p.shape[0]
    n = batch * seq
    assert a_q_gain.shape[1] == HEAD_DIM and b_q_gain.shape[1] == HEAD_DIM and seq % GRID_W == 0
    bf = lambda w: w.astype(_BF16)
    lanes = lambda g, scale: jnp.tile(g.astype(_F32), d // g.shape[0]).reshape(1, d) * scale

    rel = seq - jnp.arange(2 * seq, dtype=jnp.int32)
    in_bucket = _t5_bucket(rel)[None, :, None] == jnp.arange(T5_BUCKETS, dtype=jnp.int32)[None, None, :]
    rel_bias = jnp.sum(jnp.where(in_bucket, t5_table.astype(_F32).T[:, None, :] * LOG2E, 0.0), axis=-1)
    rel_bias = rel_bias.reshape(-1, 1, 2 * seq)

    w_in, w_out = bf(w_ffn_in), bf(w_ffn_out)
    w_gate, w_proj = bf(w_ple_gate), bf(w_ple_proj)
    qkv_a, qkv_b, wo_a, wo_b = bf(a_w_qkv), bf(b_w_qkv), bf(a_w_o), bf(b_w_o)
    p_rows = p.reshape(depth * n, -1)

    h = x.reshape(n, d)
    for i in range(depth):
        j = i // 2
        h = _mlp_call(h, norm_gains[i, 0], w_in, w_out, (i, 0))
        if i % 2 == 0:
            heads = d // LANES
            q, k, vt = _qkv_call(h, norm_gains[i, 1], qkv_a, j,
                                 lanes(a_q_gain[j], HEAD_DIM ** -0.5 * LOG2E), lanes(a_k_gain[j], 1.0),
                                 batch=batch, keys=ATTN_TILE)
            lambda_init = 0.8 - 0.6 * math.exp(-0.3 * i)
            attend = functools.partial(_diff_attn_call, batch=batch, seq=seq, heads=heads,
                                       lambda_init=lambda_init)
            bounded = _scores_bounded(a_q_gain[j], a_k_gain[j], t5_table)
            o = lax.cond(bounded, functools.partial(attend, stable=False), functools.partial(attend, stable=True),
                         q, k, vt, rel_bias, a_lambda[j], a_subln_gain[j])
            w_o = wo_a
        else:
            q, k, vt = _qkv_call(h, norm_gains[i, 1], qkv_b, j,
                                 lanes(b_q_gain[j], HEAD_DIM ** -0.5 * LOG2E), lanes(b_k_gain[j], 1.0),
                                 batch=batch, keys=2 * GRID_W)
            attend = functools.partial(_natten_call, batch=batch, seq=seq)
            bounded = _scores_bounded(b_q_gain[j], b_k_gain[j], b_rpb[j])
            o = lax.cond(bounded, functools.partial(attend, stable=False), functools.partial(attend, stable=True),
                         q, k, vt, _natten_table(b_rpb[j]))
            w_o = wo_b
        h = _mlp_call(h, norm_gains[i, 2], w_in, w_out, (i, 1), proj=(o, w_o, j),
                      ple=(norm_gains[i, 3], w_gate, b_ple_gate[i], p_rows, w_proj, i))
    return h.reshape(batch, seq, d)
```

```python
import functools
import math

import jax
import jax.numpy as jnp
from jax import lax
from jax.experimental import pallas as pl
from jax.experimental.pallas import tpu as pltpu

EPS = 1e-6
LOG2E = math.log2(math.e)
NEG_BIG = -1e30
LANES = 128
HEAD_DIM = 64
VMEM_LIMIT = 56 * 1024 * 1024

T5_BUCKETS = 32
T5_MAX_DIST = 128
GRID_W = 64
NA_ROWS = 8
NA_COLS = 16

TOKEN_TILE = 512
FF_CHUNK = 256
ATTN_TILE = 512
QUERY_COLS = 256
Q_PER_STEP = 4
NEAR_BLOCKS = 3
SAFE_EXP2_RANGE = 90.0
SUM_ROWS = 16

_F32 = jnp.float32
_BF16 = jnp.bfloat16


def _rms(x, gain):
    return x * lax.rsqrt(jnp.mean(x * x, axis=-1, keepdims=True) + EPS) * gain


def _dot(a, b):
    return jnp.dot(a, b, preferred_element_type=_F32)


def _dot_nt(a, b):
    return lax.dot_general(a, b, (((1,), (1,)), ((), ())), preferred_element_type=_F32)


def _resident(arr, lead=()):
    tail = arr.shape[len(lead):]
    index = tuple(lead) + (0,) * len(tail)
    return pl.BlockSpec((None,) * len(lead) + tail, lambda *_: index, pipeline_mode=pl.Buffered(1))


def _head_rms_store(x, gain_ref, out_ref):
    low = lax.broadcasted_iota(jnp.int32, (1, LANES), 1) < HEAD_DIM
    for j in range(x.shape[1] // LANES):
        xb = x[:, j * LANES:(j + 1) * LANES]
        sq = xb * xb
        s_lo = jnp.sum(jnp.where(low, sq, 0.0), axis=-1, keepdims=True)
        s_hi = jnp.sum(jnp.where(low, 0.0, sq), axis=-1, keepdims=True)
        ms = jnp.where(low, s_lo, s_hi) * (1.0 / HEAD_DIM)
        yb = xb * lax.rsqrt(ms + EPS) * gain_ref[:, j * LANES:(j + 1) * LANES]
        out_ref[:, j * LANES:(j + 1) * LANES] = yb.astype(out_ref.dtype)


def _qkv_store(y, g_ref, w_ref, qg_ref, kg_ref, q_ref, k_ref, vt_ref):
    d = y.shape[1]
    xn = _rms(y, g_ref[...]).astype(_BF16)
    _head_rms_store(_dot(xn, w_ref[:, 0:d]), qg_ref, q_ref)
    _head_rms_store(_dot(xn, w_ref[:, d:2 * d]), kg_ref, k_ref)
    v = _dot(xn, w_ref[:, 2 * d:3 * d])
    groups, keys = vt_ref.shape[2], vt_ref.shape[4]
    ones = jnp.ones((SUM_ROWS, keys), vt_ref.dtype)
    for hb in range(d // LANES):
        vt = v[:, hb * LANES:(hb + 1) * LANES].T.astype(vt_ref.dtype)
        for t in range(groups):
            vt_ref[0, hb, t, :LANES, :] = vt[:, t * keys:(t + 1) * keys]
            vt_ref[0, hb, t, LANES:, :] = ones


def _token_kernel(*refs, has_proj, has_ple, has_qkv, d_ff):
    it = iter(refs)
    h_ref = next(it)
    if has_proj:
        o_ref, wo_ref = next(it), next(it)
    g_ref, win_ref, wout_ref = next(it), next(it), next(it)
    if has_ple:
        g3_ref, wg_ref, bg_ref, p_ref, wp_ref = (next(it) for _ in range(5))
    if has_qkv:
        qkv_in = [next(it) for _ in range(4)]
    out_ref = next(it)

    x = h_ref[...]
    if has_proj:
        x = x + _dot(o_ref[...], wo_ref[...])
    xn = _rms(x, g_ref[...]).astype(_BF16)
    acc = jnp.zeros_like(x)
    for c in range(d_ff // FF_CHUNK):
        lo = c * FF_CHUNK
        g = _dot(xn, win_ref[:, lo:lo + FF_CHUNK])
        u = _dot(xn, win_ref[:, d_ff + lo:d_ff + lo + FF_CHUNK])
        a = (g * jax.nn.sigmoid(g) * u).astype(_BF16)
        acc = acc + _dot(a, wout_ref[lo:lo + FF_CHUNK, :])
    y = x + 0.5 * acc
    if has_ple:
        yn = _rms(y, g3_ref[...]).astype(_BF16)
        gate = jax.nn.sigmoid(_dot(yn, wg_ref[...]) + bg_ref[...])
        y = y + gate * _dot(p_ref[...].astype(_BF16), wp_ref[...])
    out_ref[...] = y
    if has_qkv:
        _qkv_store(y, *qkv_in, *it)


def _token_call(h, gain, w_in, w_out, lead, proj=None, ple=None, qkv=None):
    n, d = h.shape
    d_ff = w_out.shape[-2]
    assert n % TOKEN_TILE == 0 and d_ff % FF_CHUNK == 0
    tiles = n // TOKEN_TILE
    row = lambda width, first=0: pl.BlockSpec((TOKEN_TILE, width), lambda i: (first + i, 0))
    vec = lambda v: v.reshape(1, d)
    args, specs = [h], [row(d)]
    if proj is not None:
        o, w_o, j = proj
        args += [o, w_o]
        specs += [row(o.shape[1]), _resident(w_o, (j,))]
    args += [vec(gain), w_in, w_out]
    specs += [_resident(vec(gain)), _resident(w_in, lead), _resident(w_out, lead)]
    if ple is not None:
        g3, w_g, b_g, p_rows, w_p, layer = ple
        args += [vec(g3), w_g, vec(b_g), p_rows, w_p]
        specs += [_resident(vec(g3)), _resident(w_g, (layer,)), _resident(vec(b_g)),
                  row(p_rows.shape[1], layer * tiles), _resident(w_p, (layer,))]
    out_specs, out_shape = [row(d)], [jax.ShapeDtypeStruct((n, d), _F32)]
    if qkv is not None:
        g1, w_qkv, j, q_gain_lanes, k_gain_lanes, batch, keys = qkv
        seq = n // batch
        assert seq % TOKEN_TILE == 0 and TOKEN_TILE % keys == 0
        per_batch, groups = seq // TOKEN_TILE, TOKEN_TILE // keys
        args += [vec(g1), w_qkv, q_gain_lanes, k_gain_lanes]
        specs += [_resident(vec(g1)), _resident(w_qkv, (j,)), _resident(q_gain_lanes), _resident(k_gain_lanes)]
        out_specs += [row(d), row(d),
                      pl.BlockSpec((1, d // LANES, groups, LANES + SUM_ROWS, keys),
                                   lambda i: (i // per_batch, 0, i % per_batch, 0, 0))]
        out_shape += [jax.ShapeDtypeStruct((n, d), _BF16)] * 2
        out_shape += [jax.ShapeDtypeStruct((batch, d // LANES, seq // keys, LANES + SUM_ROWS, keys), _BF16)]
    kern = functools.partial(_token_kernel, has_proj=proj is not None, has_ple=ple is not None,
                             has_qkv=qkv is not None, d_ff=d_ff)
    return pl.pallas_call(
        kern,
        grid=(tiles,),
        in_specs=specs,
        out_specs=out_specs,
        out_shape=out_shape,
        compiler_params=pltpu.CompilerParams(dimension_semantics=("arbitrary",),
                                             vmem_limit_bytes=VMEM_LIMIT),
        name="ffn_qkv" if qkv is not None else "proj_ffn_ple",
    )(*args)


def _diff_attn_kernel(q_ref, k_ref, vt_ref, rb_ref, lam_ref, sg_ref, o_ref,
                      bias_ref, qt_ref, acc_ref, *maybe_m_ref, seq, tile, lambda_init, stable):
    b, step = pl.program_id(1), pl.program_id(2)
    n_blocks = seq // tile

    @pl.when((b == 0) & (step == 0))
    def _():
        for t in range(-2, 3):
            w = rb_ref[0, :, seq - (t + 1) * tile:seq - (t - 1) * tile]
            rolled = pltpu.roll(jnp.broadcast_to(w, (tile, 2 * tile)), 0, 1, stride=1, stride_axis=0)
            bias_ref[t + 2] = rolled[:, tile:]

    low = lax.broadcasted_iota(jnp.int32, (LANES, 1), 0) < HEAD_DIM
    col_slices = [slice(c * QUERY_COLS, (c + 1) * QUERY_COLS) for c in range(2 * tile // QUERY_COLS)]
    bias_before = rb_ref[0, :, 2 * seq - 1:2 * seq]
    bias_after = rb_ref[0, :, 1:2]

    def prepare(u):
        qt = q_ref[u * tile:(u + 1) * tile, :].astype(_F32).T
        qt_ref[u, :, :tile] = jnp.where(low, qt, 0.0).astype(_BF16)
        qt_ref[u, :, tile:] = jnp.where(low, 0.0, qt).astype(_BF16)
        if stable:
            maybe_m_ref[0][u] = jnp.full(maybe_m_ref[0].shape[1:], NEG_BIG, _F32)
        acc_ref[u] = jnp.zeros(acc_ref.shape[1:], _F32)

    def block_order(i):
        first_near = jnp.clip(i - 1, 0, n_blocks - NEAR_BLOCKS)
        order = []
        for n in range(n_blocks):
            if n < NEAR_BLOCKS:
                order.append((first_near + n, True, None))
            else:
                f = n - NEAR_BLOCKS
                before = f < first_near
                order.append((jnp.where(before, f, f + NEAR_BLOCKS), False,
                              jnp.where(before, bias_before, bias_after)))
        return order

    def scores(u, i, j, near, cols):
        kb = k_ref[pl.ds(pl.multiple_of(j * tile, tile), tile), :]
        s = _dot(kb, qt_ref[u, :, cols])
        if near:
            bcol = cols.start % tile
            s = s + bias_ref[jnp.clip(j - i, -2, 2) + 2, :, bcol:bcol + QUERY_COLS]
        return s

    def softmax_update(u, s, j, const_bias, cols):
        if not stable:
            p = jnp.exp2(s)
            pv = jnp.concatenate([_dot(vt_ref[0, 0, j, :LANES, :], p.astype(_BF16)),
                                  jnp.broadcast_to(jnp.sum(p, axis=0, keepdims=True), (SUM_ROWS, QUERY_COLS))],
                                 axis=0)
            if const_bias is not None:
                pv = pv * jnp.exp2(const_bias)
            acc_ref[u, :, cols] += pv
            return
        m_ref = maybe_m_ref[0]
        m_old = m_ref[u, :, cols]
        col_max = jnp.max(s, axis=0, keepdims=True)
        if const_bias is not None:
            col_max = col_max + const_bias
        m_new = jnp.maximum(m_old, col_max)
        shift = m_new if const_bias is None else m_new - const_bias
        p = jnp.exp2(s - shift).astype(_BF16)
        m_ref[u, :, cols] = m_new
        acc_ref[u, :, cols] = jnp.exp2(m_old - m_new) * acc_ref[u, :, cols] + _dot(vt_ref[0, 0, j], p)

    def finish(u, lam):
        a = acc_ref[u, :LANES, :] / acc_ref[u, LANES:LANES + 1, :]
        ot = a[:, :tile] - lam * a[:, tile:]
        ms = jnp.mean(ot * ot, axis=0, keepdims=True)
        ot = ot * lax.rsqrt(ms + EPS) * (sg_ref[...] * (1.0 - lambda_init))
        o_ref[u * tile:(u + 1) * tile, :] = ot.T.astype(o_ref.dtype)

    streams = range(Q_PER_STEP)
    blocks = [Q_PER_STEP * step + u for u in streams]
    orders = [block_order(i) for i in blocks]
    for u in streams:
        prepare(u)
    ahead = [[scores(u, blocks[u], orders[u][0][0], orders[u][0][1], cols) for cols in col_slices]
             for u in streams]
    for n in range(n_blocks):
        for c, cols in enumerate(col_slices):
            for u in streams:
                j, _, const_bias = orders[u][n]
                current = ahead[u][c]
                if n + 1 < n_blocks:
                    ahead[u][c] = scores(u, blocks[u], orders[u][n + 1][0], orders[u][n + 1][1], cols)
                softmax_update(u, current, j, const_bias, cols)

    lp = lam_ref[...]
    lam = (jnp.exp(jnp.sum(lp[0:1] * lp[1:2], axis=-1, keepdims=True))
           - jnp.exp(jnp.sum(lp[2:3] * lp[3:4], axis=-1, keepdims=True)) + lambda_init)
    for u in streams:
        finish(u, lam)


def _diff_attn_call(q, k, vt, rel_bias, lam_p, subln_gain, *, batch, seq, heads, lambda_init, stable):
    n, d = q.shape
    tile = ATTN_TILE
    assert seq % tile == 0 and tile >= T5_MAX_DIST and seq >= 3 * tile and d == heads * LANES
    assert tile % QUERY_COLS == 0 and seq // tile >= NEAR_BLOCKS
    nq = seq // tile
    assert vt.shape == (batch, heads, nq, LANES + SUM_ROWS, tile) and nq % Q_PER_STEP == 0
    steps = nq // Q_PER_STEP
    q_spec = pl.BlockSpec((Q_PER_STEP * tile, LANES), lambda h, b, i: (b * steps + i, h))
    kern = functools.partial(_diff_attn_kernel, seq=seq, tile=tile, lambda_init=lambda_init, stable=stable)
    running_max = [pltpu.VMEM((Q_PER_STEP, 1, 2 * tile), _F32)] if stable else []
    return pl.pallas_call(
        kern,
        grid=(heads, batch, steps),
        in_specs=[q_spec,
                  pl.BlockSpec((seq, LANES), lambda h, b, i: (b, h)),
                  pl.BlockSpec((1, 1, nq, LANES + SUM_ROWS, tile), lambda h, b, i: (b, h, 0, 0, 0)),
                  pl.BlockSpec((1, 1, 2 * seq), lambda h, b, i: (h, 0, 0)),
                  pl.BlockSpec(lam_p.shape, lambda h, b, i: (0, 0)),
                  pl.BlockSpec((LANES, 1), lambda h, b, i: (0, 0))],
        out_specs=q_spec,
        out_shape=jax.ShapeDtypeStruct((n, d), _BF16),
        scratch_shapes=[pltpu.VMEM((5, tile, tile), _F32),
                        pltpu.VMEM((Q_PER_STEP, LANES, 2 * tile), _BF16),
                        pltpu.VMEM((Q_PER_STEP, LANES + SUM_ROWS, 2 * tile), _F32),
                        ] + running_max,
        compiler_params=pltpu.CompilerParams(dimension_semantics=("arbitrary",) * 3,
                                             vmem_limit_bytes=VMEM_LIMIT),
        name="diff_attn" if stable else "diff_attn_bounded",
    )(q, k, vt, rel_bias, lam_p, subln_gain.reshape(LANES, 1))


def _t5_bucket(rel):
    half = T5_BUCKETS // 2
    max_exact = half // 2
    ret = jnp.where(rel > 0, half, 0)
    n = jnp.abs(rel)
    nf = jnp.maximum(n, 1).astype(_F32)
    large = max_exact + (jnp.log(nf / max_exact) / math.log(T5_MAX_DIST / max_exact)
                         * (half - max_exact)).astype(jnp.int32)
    large = jnp.minimum(large, half - 1)
    return ret + jnp.where(n < max_exact, n, large)


NA_SLAB_ROWS = NA_ROWS + 2
NA_EDGE_STEPS = 2
INVALID = 2 * NA_ROWS - 1
NA_UNROLL = 16


def _na_row_start(r, rows):
    return min(max(r - NA_ROWS // 2, 0), rows - NA_ROWS)


def _natten_kernel(q_ref, k_ref, vt_ref, tab_ref, o_ref, blk_ref, bias_ref, *, rows, stable):
    w = GRID_W
    b = pl.program_id(1)
    lane = lax.broadcasted_iota(jnp.int32, (w, LANES), 1)
    low_lane = lane < w

    @pl.when(b == 0)
    def _():
        key_col = lax.broadcasted_iota(jnp.int32, (w, LANES), 0)
        col_start = jnp.clip((lane & (w - 1)) - NA_COLS // 2, 0, w - NA_COLS)
        in_window = (key_col >= col_start) & (key_col < col_start + NA_COLS)
        for e in range(2):
            for half in range(2):
                for dr in range(INVALID):
                    row = jnp.broadcast_to(tab_ref[e, half, dr:dr + 1, :], (w, LANES))
                    rolled = pltpu.roll(row, 0, 1, stride=1, stride_axis=0)
                    blk_ref[e, half, dr] = jnp.where(in_window, rolled, NEG_BIG)
                blk_ref[e, half, INVALID] = jnp.full((w, LANES), NEG_BIG, _F32)
        for variant, r0 in enumerate(_na_variant_rows(rows)):
            kstart = min(_na_row_start(r0, rows), rows - NA_SLAB_ROWS)
            for e in range(2):
                for dk in range(NA_SLAB_ROWS):
                    idx = []
                    for dq in range(2):
                        rs = _na_row_start(r0 + dq, rows)
                        inside = rs <= kstart + dk < rs + NA_ROWS
                        idx.append(kstart + dk - (r0 + dq) + NA_ROWS - 1 if inside else INVALID)
                    bias_ref[variant, dk * w:(dk + 1) * w, e * LANES:(e + 1) * LANES] = jnp.where(
                        low_lane, blk_ref[e, 0, idx[0]], blk_ref[e, 1, idx[1]])

    low_row = lax.broadcasted_iota(jnp.int32, (LANES, 1), 0) < HEAD_DIM
    n_steps = rows // 2

    def key_start(g):
        return jnp.minimum(jnp.clip(2 * g - NA_ROWS // 2, 0, rows - NA_ROWS), rows - NA_SLAB_ROWS)

    def query_rows(g):
        return pl.ds(pl.multiple_of(2 * g * w, 2 * w), 2 * w)

    def scores(g):
        variant = jnp.where(g < NA_EDGE_STEPS, g,
                            jnp.where(g >= n_steps - NA_EDGE_STEPS, g - (n_steps - 2 * NA_EDGE_STEPS - 1),
                                      NA_EDGE_STEPS))
        qt = q_ref[query_rows(g), :].astype(_F32).T
        qst = jnp.concatenate([jnp.where(low_row, qt, 0.0), jnp.where(low_row, 0.0, qt)],
                              axis=1).astype(_BF16)
        kw = k_ref[pl.ds(pl.multiple_of(key_start(g) * w, 2 * w), NA_SLAB_ROWS * w), :]
        return _dot(kw, qst) + bias_ref[variant]

    def probs(s):
        if stable:
            return jnp.exp2(s - jnp.max(s, axis=0, keepdims=True)).astype(_BF16), None
        p = jnp.exp2(s)
        return p.astype(_BF16), jnp.sum(p, axis=0, keepdims=True)

    def output(g, p_and_sum):
        p, col_sum = p_and_sum
        vt = vt_ref[0, 0, pl.ds(key_start(g) // 2, NA_SLAB_ROWS // 2)]
        if stable:
            vt = jnp.concatenate([vt[t] for t in range(NA_SLAB_ROWS // 2)], axis=1)
            ot = _dot(vt, p)
            a = ot[:LANES] / ot[LANES:LANES + 1]
        else:
            vt = jnp.concatenate([vt[t, :LANES, :] for t in range(NA_SLAB_ROWS // 2)], axis=1)
            a = _dot(vt, p) / col_sum
        o_ref[query_rows(g), :] = jnp.where(low_row, a[:, :LANES], a[:, LANES:]).T.astype(o_ref.dtype)

    def body(gg, carry):
        steps = [NA_UNROLL * gg + u for u in range(NA_UNROLL)]
        all_scores = [scores(g) for g in steps]
        all_probs = [probs(s) for s in all_scores]
        for g, p in zip(steps, all_probs):
            output(g, p)
        return carry

    lax.fori_loop(0, n_steps // NA_UNROLL, body, 0)


def _na_variant_rows(rows):
    lead = [2 * g for g in range(NA_EDGE_STEPS)]
    trail = [rows - 2 * NA_EDGE_STEPS + 2 * g for g in range(NA_EDGE_STEPS)]
    return lead + [2 * NA_EDGE_STEPS] + trail


def _natten_call(q, k, vt, tab, *, batch, seq, stable):
    n, d = q.shape
    rows = seq // GRID_W
    pairs = d // LANES
    assert GRID_W * 2 == LANES and NA_ROWS // 2 <= 2 * NA_EDGE_STEPS and rows % 4 == 0
    assert rows >= NA_SLAB_ROWS + 4 * NA_EDGE_STEPS
    assert vt.shape == (batch, pairs, rows // 2, LANES + SUM_ROWS, LANES)
    spec = pl.BlockSpec((seq, LANES), lambda j, b: (b, j))
    n_variants = 2 * NA_EDGE_STEPS + 1
    return pl.pallas_call(
        functools.partial(_natten_kernel, rows=rows, stable=stable),
        grid=(pairs, batch),
        in_specs=[spec, spec,
                  pl.BlockSpec((1, 1) + vt.shape[2:], lambda j, b: (b, j, 0, 0, 0)),
                  pl.BlockSpec((2,) + tab.shape[1:], lambda j, b: (j, 0, 0, 0))],
        out_specs=spec,
        out_shape=jax.ShapeDtypeStruct((n, d), _BF16),
        scratch_shapes=[pltpu.VMEM((2, 2, INVALID + 1, GRID_W, LANES), _F32),
                        pltpu.VMEM((n_variants, NA_SLAB_ROWS * GRID_W, 2 * LANES), _F32)],
        compiler_params=pltpu.CompilerParams(dimension_semantics=("arbitrary",) * 2,
                                             vmem_limit_bytes=VMEM_LIMIT),
        name="natten" if stable else "natten_bounded",
    )(q, k, vt, tab)


def _natten_table(rpb):
    rev = rpb.astype(_F32)[:, :, ::-1] * LOG2E
    left = GRID_W - NA_COLS + 1
    plain = jnp.pad(rev, ((0, 0), (0, 0), (left, LANES - left - rev.shape[-1])))
    return jnp.stack([jnp.roll(plain, GRID_W, axis=-1), plain], axis=1)


def _scores_bounded(q_gain, k_gain, bias_table):
    f32 = lambda a: jnp.max(jnp.abs(a.astype(_F32)))
    bound = LOG2E * (1.02 * HEAD_DIM ** 0.5 * f32(q_gain) * f32(k_gain) + f32(bias_table))
    return bound <= SAFE_EXP2_RANGE


def kernel(x, p, norm_gains, w_ffn_in, w_ffn_out, t5_table, a_w_qkv, a_w_o, a_q_gain, a_k_gain, a_lambda,
           a_subln_gain, b_w_qkv, b_w_o, b_q_gain, b_k_gain, b_rpb, w_ple_gate, b_ple_gate, w_ple_proj):
    batch, seq, d = x.shape
    depth = p.shape[0]
    n = batch * seq
    assert a_q_gain.shape[1] == HEAD_DIM and b_q_gain.shape[1] == HEAD_DIM and seq % GRID_W == 0
    bf = lambda w: w.astype(_BF16)
    lanes = lambda g, scale: jnp.tile(g.astype(_F32), d // g.shape[0]).reshape(1, d) * scale

    rel = seq - jnp.arange(2 * seq, dtype=jnp.int32)
    in_bucket = _t5_bucket(rel)[None, :, None] == jnp.arange(T5_BUCKETS, dtype=jnp.int32)[None, None, :]
    rel_bias = jnp.sum(jnp.where(in_bucket, t5_table.astype(_F32).T[:, None, :] * LOG2E, 0.0), axis=-1)
    rel_bias = rel_bias.reshape(-1, 1, 2 * seq)

    w_in, w_out = bf(w_ffn_in), bf(w_ffn_out)
    w_gate, w_proj = bf(w_ple_gate), bf(w_ple_proj)
    qkv_a, qkv_b, wo_a, wo_b = bf(a_w_qkv), bf(b_w_qkv), bf(a_w_o), bf(b_w_o)
    p_rows = p.reshape(depth * n, -1)

    h = x.reshape(n, d)
    for i in range(depth):
        j = i // 2
        q_scale = HEAD_DIM ** -0.5 * LOG2E
        if i % 2 == 0:
            heads = d // LANES
            h, q, k, vt = _token_call(h, norm_gains[i, 0], w_in, w_out, (i, 0),
                                      qkv=(norm_gains[i, 1], qkv_a, j, lanes(a_q_gain[j], q_scale),
                                           lanes(a_k_gain[j], 1.0), batch, ATTN_TILE))
            lambda_init = 0.8 - 0.6 * math.exp(-0.3 * i)
            attend = functools.partial(_diff_attn_call, batch=batch, seq=seq, heads=heads,
                                       lambda_init=lambda_init)
            bounded = _scores_bounded(a_q_gain[j], a_k_gain[j], t5_table)
            o = lax.cond(bounded, functools.partial(attend, stable=False), functools.partial(attend, stable=True),
                         q, k, vt, rel_bias, a_lambda[j], a_subln_gain[j])
            w_o = wo_a
        else:
            h, q, k, vt = _token_call(h, norm_gains[i, 0], w_in, w_out, (i, 0),
                                      qkv=(norm_gains[i, 1], qkv_b, j, lanes(b_q_gain[j], q_scale),
                                           lanes(b_k_gain[j], 1.0), batch, 2 * GRID_W))
            attend = functools.partial(_natten_call, batch=batch, seq=seq)
            bounded = _scores_bounded(b_q_gain[j], b_k_gain[j], b_rpb[j])
            o = lax.cond(bounded, functools.partial(attend, stable=False), functools.partial(attend, stable=True),
                         q, k, vt, _natten_table(b_rpb[j]))
            w_o = wo_b
        (h,) = _token_call(h, norm_gains[i, 2], w_in, w_out, (i, 1), proj=(o, w_o, j),
                           ple=(norm_gains[i, 3], w_gate, b_ple_gate[i], p_rows, w_proj, i))
    return h.reshape(batch, seq, d)
```

```python
import functools
import math

import jax
import jax.numpy as jnp
from jax import lax
from jax.experimental import pallas as pl
from jax.experimental.pallas import tpu as pltpu

EPS = 1e-6
LOG2E = math.log2(math.e)
NEG_BIG = -1e30
LANES = 128
HEAD_DIM = 64
VMEM_LIMIT = 56 * 1024 * 1024

T5_BUCKETS = 32
T5_MAX_DIST = 128
GRID_W = 64
NA_ROWS = 8
NA_COLS = 16

TOKEN_TILE = 512
FF_CHUNK = 256
ATTN_TILE = 512
QUERY_COLS = 256
Q_PER_STEP = 4
NEAR_BLOCKS = 3
SAFE_EXP2_RANGE = 90.0
SUM_ROWS = 16

_F32 = jnp.float32
_BF16 = jnp.bfloat16


def _rms(x, gain):
    return x * lax.rsqrt(jnp.mean(x * x, axis=-1, keepdims=True) + EPS) * gain


def _dot(a, b):
    return jnp.dot(a, b, preferred_element_type=_F32)


def _dot_nt(a, b):
    return lax.dot_general(a, b, (((1,), (1,)), ((), ())), preferred_element_type=_F32)


def _resident(arr, lead=()):
    tail = arr.shape[len(lead):]
    index = tuple(lead) + (0,) * len(tail)
    return pl.BlockSpec((None,) * len(lead) + tail, lambda *_: index, pipeline_mode=pl.Buffered(1))


def _head_rms_store(x, gain_ref, out_ref):
    low = lax.broadcasted_iota(jnp.int32, (1, LANES), 1) < HEAD_DIM
    for j in range(x.shape[1] // LANES):
        xb = x[:, j * LANES:(j + 1) * LANES]
        sq = xb * xb
        s_lo = jnp.sum(jnp.where(low, sq, 0.0), axis=-1, keepdims=True)
        s_hi = jnp.sum(jnp.where(low, 0.0, sq), axis=-1, keepdims=True)
        ms = jnp.where(low, s_lo, s_hi) * (1.0 / HEAD_DIM)
        yb = xb * lax.rsqrt(ms + EPS) * gain_ref[:, j * LANES:(j + 1) * LANES]
        out_ref[:, j * LANES:(j + 1) * LANES] = yb.astype(out_ref.dtype)


def _qkv_store(y, g_ref, w_ref, qg_ref, kg_ref, q_ref, k_ref, vt_ref):
    d = y.shape[1]
    xn = _rms(y, g_ref[...]).astype(_BF16)
    _head_rms_store(_dot(xn, w_ref[:, 0:d]), qg_ref, q_ref)
    _head_rms_store(_dot(xn, w_ref[:, d:2 * d]), kg_ref, k_ref)
    v = _dot(xn, w_ref[:, 2 * d:3 * d])
    groups, keys = vt_ref.shape[2], vt_ref.shape[4]
    ones = jnp.ones((SUM_ROWS, keys), vt_ref.dtype)
    for hb in range(d // LANES):
        vt = v[:, hb * LANES:(hb + 1) * LANES].T.astype(vt_ref.dtype)
        for t in range(groups):
            vt_ref[0, hb, t, :LANES, :] = vt[:, t * keys:(t + 1) * keys]
            vt_ref[0, hb, t, LANES:, :] = ones


def _token_kernel(*refs, has_proj, has_ple, has_qkv, n_casts, d_ff):
    it = iter(refs)
    h_ref = next(it)
    if has_proj:
        o_ref, wo_ref = next(it), next(it)
    g_ref, win_ref, wout_ref = next(it), next(it), next(it)
    if has_ple:
        g3_ref, wg_ref, bg_ref, p_ref, wp_ref = (next(it) for _ in range(5))
    if has_qkv:
        qkv_in = [next(it) for _ in range(4)]
    cast_src = [next(it) for _ in range(n_casts)]
    out_ref = next(it)

    x = h_ref[...]
    if has_proj:
        x = x + _dot(o_ref[...], wo_ref[...])
    xn = _rms(x, g_ref[...]).astype(_BF16)
    acc = jnp.zeros_like(x)
    for c in range(d_ff // FF_CHUNK):
        lo = c * FF_CHUNK
        g = _dot(xn, win_ref[:, lo:lo + FF_CHUNK])
        u = _dot(xn, win_ref[:, d_ff + lo:d_ff + lo + FF_CHUNK])
        a = (g * jax.nn.sigmoid(g) * u).astype(_BF16)
        acc = acc + _dot(a, wout_ref[lo:lo + FF_CHUNK, :])
    y = x + 0.5 * acc
    if has_ple:
        yn = _rms(y, g3_ref[...]).astype(_BF16)
        gate = jax.nn.sigmoid(_dot(yn, wg_ref[...]) + bg_ref[...])
        y = y + gate * _dot(p_ref[...].astype(_BF16), wp_ref[...])
    out_ref[...] = y
    if has_qkv:
        _qkv_store(y, *qkv_in, *(next(it) for _ in range(3)))
    for src_ref in cast_src:
        dst_ref = next(it)
        dst_ref[...] = src_ref[...].astype(dst_ref.dtype)


def _cast_chunks(w, lead, steps):
    rows, cols = w.shape[len(lead):]
    chunks = next(c for c in (steps, steps // 2, steps // 4) if rows % c == 0 and (rows // c) % 16 == 0)
    block = rows // chunks
    src = pl.BlockSpec((None,) * len(lead) + (block, cols),
                       lambda i: tuple(lead) + (jnp.minimum(i, chunks - 1), 0))
    dst = pl.BlockSpec((block, cols), lambda i: (jnp.minimum(i, chunks - 1), 0))
    return src, dst, jax.ShapeDtypeStruct((rows, cols), _BF16)


def _token_call(h, gain, w_in, w_out, lead, proj=None, ple=None, qkv=None, cast=()):
    n, d = h.shape
    d_ff = w_out.shape[-2]
    assert n % TOKEN_TILE == 0 and d_ff % FF_CHUNK == 0
    tiles = n // TOKEN_TILE
    row = lambda width, first=0: pl.BlockSpec((TOKEN_TILE, width), lambda i: (first + i, 0))
    vec = lambda v: v.reshape(1, d)
    args, specs = [h], [row(d)]
    if proj is not None:
        o, w_o, j = proj
        args += [o, w_o]
        specs += [row(o.shape[1]), _resident(w_o, (j,))]
    args += [vec(gain), w_in, w_out]
    specs += [_resident(vec(gain)), _resident(w_in, lead), _resident(w_out, lead)]
    if ple is not None:
        g3, w_g, b_g, p_rows, w_p, layer = ple
        args += [vec(g3), w_g, vec(b_g), p_rows, w_p]
        specs += [_resident(vec(g3)), _resident(w_g, (layer,)), _resident(vec(b_g)),
                  row(p_rows.shape[1], layer * tiles), _resident(w_p, (layer,))]
    out_specs, out_shape = [row(d)], [jax.ShapeDtypeStruct((n, d), _F32)]
    if qkv is not None:
        g1, w_qkv, qkv_lead, q_gain_lanes, k_gain_lanes, batch, keys = qkv
        seq = n // batch
        assert seq % TOKEN_TILE == 0 and TOKEN_TILE % keys == 0
        per_batch, groups = seq // TOKEN_TILE, TOKEN_TILE // keys
        args += [vec(g1), w_qkv, q_gain_lanes, k_gain_lanes]
        specs += [_resident(vec(g1)), _resident(w_qkv, qkv_lead), _resident(q_gain_lanes), _resident(k_gain_lanes)]
        out_specs += [row(d), row(d),
                      pl.BlockSpec((1, d // LANES, groups, LANES + SUM_ROWS, keys),
                                   lambda i: (i // per_batch, 0, i % per_batch, 0, 0))]
        out_shape += [jax.ShapeDtypeStruct((n, d), _BF16)] * 2
        out_shape += [jax.ShapeDtypeStruct((batch, d // LANES, seq // keys, LANES + SUM_ROWS, keys), _BF16)]
    for w, w_lead in cast:
        src_spec, dst_spec, dst_shape = _cast_chunks(w, w_lead, tiles)
        args.append(w)
        specs.append(src_spec)
        out_specs.append(dst_spec)
        out_shape.append(dst_shape)
    kern = functools.partial(_token_kernel, has_proj=proj is not None, has_ple=ple is not None,
                             has_qkv=qkv is not None, n_casts=len(cast), d_ff=d_ff)
    return pl.pallas_call(
        kern,
        grid=(tiles,),
        in_specs=specs,
        out_specs=out_specs,
        out_shape=out_shape,
        compiler_params=pltpu.CompilerParams(dimension_semantics=("arbitrary",),
                                             vmem_limit_bytes=VMEM_LIMIT),
        name="ffn_qkv" if qkv is not None else "proj_ffn_ple",
    )(*args)


def _diff_attn_kernel(q_ref, k_ref, vt_ref, rb_ref, lam_ref, sg_ref, o_ref,
                      bias_ref, qt_ref, acc_ref, *maybe_m_ref, seq, tile, lambda_init, stable):
    b, step = pl.program_id(1), pl.program_id(2)
    n_blocks = seq // tile

    @pl.when((b == 0) & (step == 0))
    def _():
        for t in range(-2, 3):
            w = rb_ref[0, :, seq - (t + 1) * tile:seq - (t - 1) * tile]
            rolled = pltpu.roll(jnp.broadcast_to(w, (tile, 2 * tile)), 0, 1, stride=1, stride_axis=0)
            bias_ref[t + 2] = rolled[:, tile:]

    low = lax.broadcasted_iota(jnp.int32, (LANES, 1), 0) < HEAD_DIM
    col_slices = [slice(c * QUERY_COLS, (c + 1) * QUERY_COLS) for c in range(2 * tile // QUERY_COLS)]
    bias_before = rb_ref[0, :, 2 * seq - 1:2 * seq]
    bias_after = rb_ref[0, :, 1:2]

    def prepare(u):
        qt = q_ref[u * tile:(u + 1) * tile, :].astype(_F32).T
        qt_ref[u, :, :tile] = jnp.where(low, qt, 0.0).astype(_BF16)
        qt_ref[u, :, tile:] = jnp.where(low, 0.0, qt).astype(_BF16)
        if stable:
            maybe_m_ref[0][u] = jnp.full(maybe_m_ref[0].shape[1:], NEG_BIG, _F32)
        acc_ref[u] = jnp.zeros(acc_ref.shape[1:], _F32)

    def block_order(i):
        first_near = jnp.clip(i - 1, 0, n_blocks - NEAR_BLOCKS)
        order = []
        for n in range(n_blocks):
            if n < NEAR_BLOCKS:
                order.append((first_near + n, True, None))
            else:
                f = n - NEAR_BLOCKS
                before = f < first_near
                order.append((jnp.where(before, f, f + NEAR_BLOCKS), False,
                              jnp.where(before, bias_before, bias_after)))
        return order

    def scores(u, i, j, near, cols):
        kb = k_ref[pl.ds(pl.multiple_of(j * tile, tile), tile), :]
        s = _dot(kb, qt_ref[u, :, cols])
        if near:
            bcol = cols.start % tile
            s = s + bias_ref[jnp.clip(j - i, -2, 2) + 2, :, bcol:bcol + QUERY_COLS]
        return s

    def softmax_update(u, s, j, const_bias, cols):
        if not stable:
            p = jnp.exp2(s)
            pv = jnp.concatenate([_dot(vt_ref[0, 0, j, :LANES, :], p.astype(_BF16)),
                                  jnp.broadcast_to(jnp.sum(p, axis=0, keepdims=True), (SUM_ROWS, QUERY_COLS))],
                                 axis=0)
            if const_bias is not None:
                pv = pv * jnp.exp2(const_bias)
            acc_ref[u, :, cols] += pv
            return
        m_ref = maybe_m_ref[0]
        m_old = m_ref[u, :, cols]
        col_max = jnp.max(s, axis=0, keepdims=True)
        if const_bias is not None:
            col_max = col_max + const_bias
        m_new = jnp.maximum(m_old, col_max)
        shift = m_new if const_bias is None else m_new - const_bias
        p = jnp.exp2(s - shift).astype(_BF16)
        m_ref[u, :, cols] = m_new
        acc_ref[u, :, cols] = jnp.exp2(m_old - m_new) * acc_ref[u, :, cols] + _dot(vt_ref[0, 0, j], p)

    def finish(u, lam):
        a = acc_ref[u, :LANES, :] / acc_ref[u, LANES:LANES + 1, :]
        ot = a[:, :tile] - lam * a[:, tile:]
        ms = jnp.mean(ot * ot, axis=0, keepdims=True)
        ot = ot * lax.rsqrt(ms + EPS) * (sg_ref[...] * (1.0 - lambda_init))
        o_ref[u * tile:(u + 1) * tile, :] = ot.T.astype(o_ref.dtype)

    streams = range(Q_PER_STEP)
    blocks = [Q_PER_STEP * step + u for u in streams]
    orders = [block_order(i) for i in blocks]
    for u in streams:
        prepare(u)
    ahead = [[scores(u, blocks[u], orders[u][0][0], orders[u][0][1], cols) for cols in col_slices]
             for u in streams]
    for n in range(n_blocks):
        for c, cols in enumerate(col_slices):
            for u in streams:
                j, _, const_bias = orders[u][n]
                current = ahead[u][c]
                if n + 1 < n_blocks:
                    ahead[u][c] = scores(u, blocks[u], orders[u][n + 1][0], orders[u][n + 1][1], cols)
                softmax_update(u, current, j, const_bias, cols)

    lp = lam_ref[...]
    lam = (jnp.exp(jnp.sum(lp[0:1] * lp[1:2], axis=-1, keepdims=True))
           - jnp.exp(jnp.sum(lp[2:3] * lp[3:4], axis=-1, keepdims=True)) + lambda_init)
    for u in streams:
        finish(u, lam)


def _diff_attn_call(q, k, vt, rel_bias, lam_p, subln_gain, *, batch, seq, heads, lambda_init, stable):
    n, d = q.shape
    tile = ATTN_TILE
    assert seq % tile == 0 and tile >= T5_MAX_DIST and seq >= 3 * tile and d == heads * LANES
    assert tile % QUERY_COLS == 0 and seq // tile >= NEAR_BLOCKS
    nq = seq // tile
    assert vt.shape == (batch, heads, nq, LANES + SUM_ROWS, tile) and nq % Q_PER_STEP == 0
    steps = nq // Q_PER_STEP
    q_spec = pl.BlockSpec((Q_PER_STEP * tile, LANES), lambda h, b, i: (b * steps + i, h))
    kern = functools.partial(_diff_attn_kernel, seq=seq, tile=tile, lambda_init=lambda_init, stable=stable)
    running_max = [pltpu.VMEM((Q_PER_STEP, 1, 2 * tile), _F32)] if stable else []
    return pl.pallas_call(
        kern,
        grid=(heads, batch, steps),
        in_specs=[q_spec,
                  pl.BlockSpec((seq, LANES), lambda h, b, i: (b, h)),
                  pl.BlockSpec((1, 1, nq, LANES + SUM_ROWS, tile), lambda h, b, i: (b, h, 0, 0, 0)),
                  pl.BlockSpec((1, 1, 2 * seq), lambda h, b, i: (h, 0, 0)),
                  pl.BlockSpec(lam_p.shape, lambda h, b, i: (0, 0)),
                  pl.BlockSpec((LANES, 1), lambda h, b, i: (0, 0))],
        out_specs=q_spec,
        out_shape=jax.ShapeDtypeStruct((n, d), _BF16),
        scratch_shapes=[pltpu.VMEM((5, tile, tile), _F32),
                        pltpu.VMEM((Q_PER_STEP, LANES, 2 * tile), _BF16),
                        pltpu.VMEM((Q_PER_STEP, LANES + SUM_ROWS, 2 * tile), _F32),
                        ] + running_max,
        compiler_params=pltpu.CompilerParams(dimension_semantics=("arbitrary",) * 3,
                                             vmem_limit_bytes=VMEM_LIMIT),
        name="diff_attn" if stable else "diff_attn_bounded",
    )(q, k, vt, rel_bias, lam_p, subln_gain.reshape(LANES, 1))


def _t5_bucket(rel):
    half = T5_BUCKETS // 2
    max_exact = half // 2
    ret = jnp.where(rel > 0, half, 0)
    n = jnp.abs(rel)
    nf = jnp.maximum(n, 1).astype(_F32)
    large = max_exact + (jnp.log(nf / max_exact) / math.log(T5_MAX_DIST / max_exact)
                         * (half - max_exact)).astype(jnp.int32)
    large = jnp.minimum(large, half - 1)
    return ret + jnp.where(n < max_exact, n, large)


NA_SLAB_ROWS = NA_ROWS + 2
NA_EDGE_STEPS = 2
INVALID = 2 * NA_ROWS - 1
NA_UNROLL = 16


def _na_row_start(r, rows):
    return min(max(r - NA_ROWS // 2, 0), rows - NA_ROWS)


def _natten_kernel(q_ref, k_ref, vt_ref, tab_ref, o_ref, blk_ref, bias_ref, *, rows, stable):
    w = GRID_W
    b = pl.program_id(1)
    lane = lax.broadcasted_iota(jnp.int32, (w, LANES), 1)
    low_lane = lane < w

    @pl.when(b == 0)
    def _():
        key_col = lax.broadcasted_iota(jnp.int32, (w, LANES), 0)
        col_start = jnp.clip((lane & (w - 1)) - NA_COLS // 2, 0, w - NA_COLS)
        in_window = (key_col >= col_start) & (key_col < col_start + NA_COLS)
        for e in range(2):
            for half in range(2):
                for dr in range(INVALID):
                    row = jnp.broadcast_to(tab_ref[e, half, dr:dr + 1, :], (w, LANES))
                    rolled = pltpu.roll(row, 0, 1, stride=1, stride_axis=0)
                    blk_ref[e, half, dr] = jnp.where(in_window, rolled, NEG_BIG)
                blk_ref[e, half, INVALID] = jnp.full((w, LANES), NEG_BIG, _F32)
        for variant, r0 in enumerate(_na_variant_rows(rows)):
            kstart = min(_na_row_start(r0, rows), rows - NA_SLAB_ROWS)
            for e in range(2):
                for dk in range(NA_SLAB_ROWS):
                    idx = []
                    for dq in range(2):
                        rs = _na_row_start(r0 + dq, rows)
                        inside = rs <= kstart + dk < rs + NA_ROWS
                        idx.append(kstart + dk - (r0 + dq) + NA_ROWS - 1 if inside else INVALID)
                    bias_ref[variant, dk * w:(dk + 1) * w, e * LANES:(e + 1) * LANES] = jnp.where(
                        low_lane, blk_ref[e, 0, idx[0]], blk_ref[e, 1, idx[1]])

    low_row = lax.broadcasted_iota(jnp.int32, (LANES, 1), 0) < HEAD_DIM
    n_steps = rows // 2

    def key_start(g):
        return jnp.minimum(jnp.clip(2 * g - NA_ROWS // 2, 0, rows - NA_ROWS), rows - NA_SLAB_ROWS)

    def query_rows(g):
        return pl.ds(pl.multiple_of(2 * g * w, 2 * w), 2 * w)

    def scores(g):
        variant = jnp.where(g < NA_EDGE_STEPS, g,
                            jnp.where(g >= n_steps - NA_EDGE_STEPS, g - (n_steps - 2 * NA_EDGE_STEPS - 1),
                                      NA_EDGE_STEPS))
        qt = q_ref[query_rows(g), :].astype(_F32).T
        qst = jnp.concatenate([jnp.where(low_row, qt, 0.0), jnp.where(low_row, 0.0, qt)],
                              axis=1).astype(_BF16)
        kw = k_ref[pl.ds(pl.multiple_of(key_start(g) * w, 2 * w), NA_SLAB_ROWS * w), :]
        return _dot(kw, qst) + bias_ref[variant]

    def probs(s):
        if stable:
            return jnp.exp2(s - jnp.max(s, axis=0, keepdims=True)).astype(_BF16), None
        p = jnp.exp2(s)
        return p.astype(_BF16), jnp.sum(p, axis=0, keepdims=True)

    def output(g, p_and_sum):
        p, col_sum = p_and_sum
        vt = vt_ref[0, 0, pl.ds(key_start(g) // 2, NA_SLAB_ROWS // 2)]
        if stable:
            vt = jnp.concatenate([vt[t] for t in range(NA_SLAB_ROWS // 2)], axis=1)
            ot = _dot(vt, p)
            a = ot[:LANES] / ot[LANES:LANES + 1]
        else:
            vt = jnp.concatenate([vt[t, :LANES, :] for t in range(NA_SLAB_ROWS // 2)], axis=1)
            a = _dot(vt, p) / col_sum
        o_ref[query_rows(g), :] = jnp.where(low_row, a[:, :LANES], a[:, LANES:]).T.astype(o_ref.dtype)

    def body(gg, carry):
        steps = [NA_UNROLL * gg + u for u in range(NA_UNROLL)]
        all_scores = [scores(g) for g in steps]
        all_probs = [probs(s) for s in all_scores]
        for g, p in zip(steps, all_probs):
            output(g, p)
        return carry

    lax.fori_loop(0, n_steps // NA_UNROLL, body, 0)


def _na_variant_rows(rows):
    lead = [2 * g for g in range(NA_EDGE_STEPS)]
    trail = [rows - 2 * NA_EDGE_STEPS + 2 * g for g in range(NA_EDGE_STEPS)]
    return lead + [2 * NA_EDGE_STEPS] + trail


def _natten_call(q, k, vt, tab, *, batch, seq, stable):
    n, d = q.shape
    rows = seq // GRID_W
    pairs = d // LANES
    assert GRID_W * 2 == LANES and NA_ROWS // 2 <= 2 * NA_EDGE_STEPS and rows % 4 == 0
    assert rows >= NA_SLAB_ROWS + 4 * NA_EDGE_STEPS
    assert vt.shape == (batch, pairs, rows // 2, LANES + SUM_ROWS, LANES)
    spec = pl.BlockSpec((seq, LANES), lambda j, b: (b, j))
    n_variants = 2 * NA_EDGE_STEPS + 1
    return pl.pallas_call(
        functools.partial(_natten_kernel, rows=rows, stable=stable),
        grid=(pairs, batch),
        in_specs=[spec, spec,
                  pl.BlockSpec((1, 1) + vt.shape[2:], lambda j, b: (b, j, 0, 0, 0)),
                  pl.BlockSpec((2,) + tab.shape[1:], lambda j, b: (j, 0, 0, 0))],
        out_specs=spec,
        out_shape=jax.ShapeDtypeStruct((n, d), _BF16),
        scratch_shapes=[pltpu.VMEM((2, 2, INVALID + 1, GRID_W, LANES), _F32),
                        pltpu.VMEM((n_variants, NA_SLAB_ROWS * GRID_W, 2 * LANES), _F32)],
        compiler_params=pltpu.CompilerParams(dimension_semantics=("arbitrary",) * 2,
                                             vmem_limit_bytes=VMEM_LIMIT),
        name="natten" if stable else "natten_bounded",
    )(q, k, vt, tab)


def _natten_table(rpb):
    rev = rpb.astype(_F32)[:, :, ::-1] * LOG2E
    left = GRID_W - NA_COLS + 1
    plain = jnp.pad(rev, ((0, 0), (0, 0), (left, LANES - left - rev.shape[-1])))
    return jnp.stack([jnp.roll(plain, GRID_W, axis=-1), plain], axis=1)


def _scores_bounded(q_gain, k_gain, bias_table):
    f32 = lambda a: jnp.max(jnp.abs(a.astype(_F32)))
    bound = LOG2E * (1.02 * HEAD_DIM ** 0.5 * f32(q_gain) * f32(k_gain) + f32(bias_table))
    return bound <= SAFE_EXP2_RANGE


def kernel(x, p, norm_gains, w_ffn_in, w_ffn_out, t5_table, a_w_qkv, a_w_o, a_q_gain, a_k_gain, a_lambda,
           a_subln_gain, b_w_qkv, b_w_o, b_q_gain, b_k_gain, b_rpb, w_ple_gate, b_ple_gate, w_ple_proj):
    batch, seq, d = x.shape
    depth = p.shape[0]
    n = batch * seq
    assert a_q_gain.shape[1] == HEAD_DIM and b_q_gain.shape[1] == HEAD_DIM and seq % GRID_W == 0
    bf = lambda w: w.astype(_BF16)
    lanes = lambda g, scale: jnp.tile(g.astype(_F32), d // g.shape[0]).reshape(1, d) * scale

    rel = seq - jnp.arange(2 * seq, dtype=jnp.int32)
    in_bucket = _t5_bucket(rel)[None, :, None] == jnp.arange(T5_BUCKETS, dtype=jnp.int32)[None, None, :]
    rel_bias = jnp.sum(jnp.where(in_bucket, t5_table.astype(_F32).T[:, None, :] * LOG2E, 0.0), axis=-1)
    rel_bias = rel_bias.reshape(-1, 1, 2 * seq)

    w_gate, w_proj, wo_a, wo_b = bf(w_ple_gate), bf(w_ple_proj), bf(a_w_o), bf(b_w_o)
    w_in, w_out, w_qkv = bf(w_ffn_in[0, 0]), bf(w_ffn_out[0, 0]), bf(a_w_qkv[0])
    p_rows = p.reshape(depth * n, -1)
    q_scale = HEAD_DIM ** -0.5 * LOG2E

    h = x.reshape(n, d)
    for i in range(depth):
        j = i // 2
        q_gain, k_gain, keys = ((a_q_gain[j], a_k_gain[j], ATTN_TILE) if i % 2 == 0 else
                                (b_q_gain[j], b_k_gain[j], 2 * GRID_W))
        h, q, k, vt, w_in, w_out = _token_call(
            h, norm_gains[i, 0], w_in, w_out, (),
            qkv=(norm_gains[i, 1], w_qkv, (), lanes(q_gain, q_scale), lanes(k_gain, 1.0), batch, keys),
            cast=[(w_ffn_in, (i, 1)), (w_ffn_out, (i, 1))])
        if i % 2 == 0:
            lambda_init = 0.8 - 0.6 * math.exp(-0.3 * i)
            attend = functools.partial(_diff_attn_call, batch=batch, seq=seq, heads=d // LANES,
                                       lambda_init=lambda_init)
            bounded = _scores_bounded(a_q_gain[j], a_k_gain[j], t5_table)
            o = lax.cond(bounded, functools.partial(attend, stable=False), functools.partial(attend, stable=True),
                         q, k, vt, rel_bias, a_lambda[j], a_subln_gain[j])
            w_o = wo_a
        else:
            attend = functools.partial(_natten_call, batch=batch, seq=seq)
            bounded = _scores_bounded(b_q_gain[j], b_k_gain[j], b_rpb[j])
            o = lax.cond(bounded, functools.partial(attend, stable=False), functools.partial(attend, stable=True),
                         q, k, vt, _natten_table(b_rpb[j]))
            w_o = wo_b
        cast = []
        if i + 1 < depth:
            next_qkv = a_w_qkv if (i + 1) % 2 == 0 else b_w_qkv
            cast = [(w_ffn_in, (i + 1, 0)), (w_ffn_out, (i + 1, 0)), (next_qkv, ((i + 1) // 2,))]
        h, *cast_out = _token_call(h, norm_gains[i, 2], w_in, w_out, (), proj=(o, w_o, j),
                                   ple=(norm_gains[i, 3], w_gate, b_ple_gate[i], p_rows, w_proj, i), cast=cast)
        if cast_out:
            w_in, w_out, w_qkv = cast_out
    return h.reshape(batch, seq, d)
```

```python
import functools
import math

import jax
import jax.numpy as jnp
from jax import lax
from jax.experimental import pallas as pl
from jax.experimental.pallas import tpu as pltpu

EPS = 1e-6
LOG2E = math.log2(math.e)
NEG_BIG = -1e30
LANES = 128
HEAD_DIM = 64
VMEM_LIMIT = 56 * 1024 * 1024

T5_BUCKETS = 32
T5_MAX_DIST = 128
GRID_W = 64
NA_ROWS = 8
NA_COLS = 16

TOKEN_TILE = 512
FF_CHUNK = 256
ATTN_TILE = 512
QUERY_COLS = 256
Q_PER_STEP = 4
NEAR_BLOCKS = 3
SAFE_EXP2_RANGE = -1.0
SUM_ROWS = 16

_F32 = jnp.float32
_BF16 = jnp.bfloat16


def _rms(x, gain):
    return x * lax.rsqrt(jnp.mean(x * x, axis=-1, keepdims=True) + EPS) * gain


def _dot(a, b):
    return jnp.dot(a, b, preferred_element_type=_F32)


def _resident(arr, lead=()):
    tail = arr.shape[len(lead):]
    index = tuple(lead) + (0,) * len(tail)
    return pl.BlockSpec((None,) * len(lead) + tail, lambda *_: index, pipeline_mode=pl.Buffered(1))


def _head_rms_store(x, gain_ref, out_ref):
    low = lax.broadcasted_iota(jnp.int32, (1, LANES), 1) < HEAD_DIM
    for j in range(x.shape[1] // LANES):
        xb = x[:, j * LANES:(j + 1) * LANES]
        sq = xb * xb
        s_lo = jnp.sum(jnp.where(low, sq, 0.0), axis=-1, keepdims=True)
        s_hi = jnp.sum(jnp.where(low, 0.0, sq), axis=-1, keepdims=True)
        ms = jnp.where(low, s_lo, s_hi) * (1.0 / HEAD_DIM)
        yb = xb * lax.rsqrt(ms + EPS) * gain_ref[:, j * LANES:(j + 1) * LANES]
        out_ref[:, j * LANES:(j + 1) * LANES] = yb.astype(out_ref.dtype)


def _qkv_store(y, g_ref, w_ref, qg_ref, kg_ref, q_ref, k_ref, vt_ref):
    d = y.shape[1]
    xn = _rms(y, g_ref[...]).astype(_BF16)
    _head_rms_store(_dot(xn, w_ref[:, 0:d]), qg_ref, q_ref)
    _head_rms_store(_dot(xn, w_ref[:, d:2 * d]), kg_ref, k_ref)
    v = _dot(xn, w_ref[:, 2 * d:3 * d])
    groups, keys = vt_ref.shape[2], vt_ref.shape[4]
    ones = jnp.ones((SUM_ROWS, keys), vt_ref.dtype)
    for hb in range(d // LANES):
        vt = v[:, hb * LANES:(hb + 1) * LANES].T.astype(vt_ref.dtype)
        for t in range(groups):
            vt_ref[0, hb, t, :LANES, :] = vt[:, t * keys:(t + 1) * keys]
            vt_ref[0, hb, t, LANES:, :] = ones


def _token_kernel(*refs, has_proj, has_ple, has_qkv, n_casts, d_ff):
    it = iter(refs)
    h_ref = next(it)
    if has_proj:
        o_ref, wo_ref = next(it), next(it)
    g_ref, win_ref, wout_ref = next(it), next(it), next(it)
    if has_ple:
        g3_ref, wg_ref, bg_ref, p_ref, wp_ref = (next(it) for _ in range(5))
    if has_qkv:
        qkv_in = [next(it) for _ in range(4)]
    cast_src = [next(it) for _ in range(n_casts)]
    out_ref, *more_out = it
    qkv_out, cast_dst = more_out[:len(more_out) - n_casts], more_out[len(more_out) - n_casts:]

    for src_ref, dst_ref in zip(cast_src, cast_dst):
        dst_ref[...] = src_ref[...].astype(dst_ref.dtype)

    x = h_ref[...]
    if has_proj:
        x = x + _dot(o_ref[...], wo_ref[...])
    xn = _rms(x, g_ref[...]).astype(_BF16)
    acc = jnp.zeros_like(x)
    for c in range(d_ff // FF_CHUNK):
        lo = c * FF_CHUNK
        g = _dot(xn, win_ref[:, lo:lo + FF_CHUNK])
        u = _dot(xn, win_ref[:, d_ff + lo:d_ff + lo + FF_CHUNK])
        a = (g * jax.nn.sigmoid(g) * u).astype(_BF16)
        acc = acc + _dot(a, wout_ref[lo:lo + FF_CHUNK, :])
    y = x + 0.5 * acc
    if has_ple:
        yn = _rms(y, g3_ref[...]).astype(_BF16)
        gate = jax.nn.sigmoid(_dot(yn, wg_ref[...]) + bg_ref[...])
        y = y + gate * _dot(p_ref[...].astype(_BF16), wp_ref[...])
    out_ref[...] = y
    if has_qkv:
        _qkv_store(y, *qkv_in, *qkv_out)


def _cast_chunks(w, lead, steps):
    rows, cols = w.shape[len(lead):]
    chunks = next(c for c in (steps, steps // 2, steps // 4) if rows % c == 0 and (rows // c) % 16 == 0)
    block = rows // chunks
    src = pl.BlockSpec((None,) * len(lead) + (block, cols),
                       lambda i: tuple(lead) + (jnp.minimum(i, chunks - 1), 0))
    dst = pl.BlockSpec((block, cols), lambda i: (jnp.minimum(i, chunks - 1), 0))
    return src, dst, jax.ShapeDtypeStruct((rows, cols), _BF16)


def _token_call(h, gain, w_in, w_out, lead, proj=None, ple=None, qkv=None, cast=()):
    n, d = h.shape
    d_ff = w_out.shape[-2]
    assert n % TOKEN_TILE == 0 and d_ff % FF_CHUNK == 0
    tiles = n // TOKEN_TILE
    row = lambda width, first=0: pl.BlockSpec((TOKEN_TILE, width), lambda i: (first + i, 0))
    vec = lambda v: v.reshape(1, d)
    args, specs = [h], [row(d)]
    if proj is not None:
        o, w_o, j = proj
        args += [o, w_o]
        specs += [row(o.shape[1]), _resident(w_o, (j,))]
    args += [vec(gain), w_in, w_out]
    specs += [_resident(vec(gain)), _resident(w_in, lead), _resident(w_out, lead)]
    if ple is not None:
        g3, w_g, b_g, p_rows, w_p, layer = ple
        args += [vec(g3), w_g, vec(b_g), p_rows, w_p]
        specs += [_resident(vec(g3)), _resident(w_g, (layer,)), _resident(vec(b_g)),
                  row(p_rows.shape[1], layer * tiles), _resident(w_p, (layer,))]
    out_specs, out_shape = [row(d)], [jax.ShapeDtypeStruct((n, d), _F32)]
    if qkv is not None:
        g1, w_qkv, qkv_lead, q_gain_lanes, k_gain_lanes, batch, keys = qkv
        seq = n // batch
        assert seq % TOKEN_TILE == 0 and TOKEN_TILE % keys == 0
        per_batch, groups = seq // TOKEN_TILE, TOKEN_TILE // keys
        args += [vec(g1), w_qkv, q_gain_lanes, k_gain_lanes]
        specs += [_resident(vec(g1)), _resident(w_qkv, qkv_lead), _resident(q_gain_lanes), _resident(k_gain_lanes)]
        out_specs += [row(d), row(d),
                      pl.BlockSpec((1, d // LANES, groups, LANES + SUM_ROWS, keys),
                                   lambda i: (i // per_batch, 0, i % per_batch, 0, 0))]
        out_shape += [jax.ShapeDtypeStruct((n, d), _BF16)] * 2
        out_shape += [jax.ShapeDtypeStruct((batch, d // LANES, seq // keys, LANES + SUM_ROWS, keys), _BF16)]
    for w, w_lead in cast:
        src_spec, dst_spec, dst_shape = _cast_chunks(w, w_lead, tiles)
        args.append(w)
        specs.append(src_spec)
        out_specs.append(dst_spec)
        out_shape.append(dst_shape)
    kern = functools.partial(_token_kernel, has_proj=proj is not None, has_ple=ple is not None,
                             has_qkv=qkv is not None, n_casts=len(cast), d_ff=d_ff)
    return pl.pallas_call(
        kern,
        grid=(tiles,),
        in_specs=specs,
        out_specs=out_specs,
        out_shape=out_shape,
        compiler_params=pltpu.CompilerParams(dimension_semantics=("arbitrary",),
                                             vmem_limit_bytes=VMEM_LIMIT),
        name="ffn_qkv" if qkv is not None else "proj_ffn_ple",
    )(*args)


def _diff_attn_kernel(q_ref, k_ref, vt_ref, rb_ref, lam_ref, sg_ref, o_ref,
                      bias_ref, qt_ref, acc_ref, *maybe_m_ref, seq, tile, lambda_init, stable):
    b, step = pl.program_id(1), pl.program_id(2)
    n_blocks = seq // tile

    @pl.when((b == 0) & (step == 0))
    def _():
        for t in range(-2, 3):
            w = rb_ref[0, :, seq - (t + 1) * tile:seq - (t - 1) * tile]
            rolled = pltpu.roll(jnp.broadcast_to(w, (tile, 2 * tile)), 0, 1, stride=1, stride_axis=0)
            bias_ref[t + 2] = rolled[:, tile:]

    low = lax.broadcasted_iota(jnp.int32, (LANES, 1), 0) < HEAD_DIM
    col_slices = [slice(c * QUERY_COLS, (c + 1) * QUERY_COLS) for c in range(2 * tile // QUERY_COLS)]
    bias_before = rb_ref[0, :, 2 * seq - 1:2 * seq]
    bias_after = rb_ref[0, :, 1:2]

    def prepare(u):
        qt = q_ref[u * tile:(u + 1) * tile, :].astype(_F32).T
        qt_ref[u, :, :tile] = jnp.where(low, qt, 0.0).astype(_BF16)
        qt_ref[u, :, tile:] = jnp.where(low, 0.0, qt).astype(_BF16)
        if stable:
            maybe_m_ref[0][u] = jnp.full(maybe_m_ref[0].shape[1:], NEG_BIG, _F32)
        acc_ref[u] = jnp.zeros(acc_ref.shape[1:], _F32)

    def block_order(i):
        first_near = jnp.clip(i - 1, 0, n_blocks - NEAR_BLOCKS)
        order = []
        for n in range(n_blocks):
            if n < NEAR_BLOCKS:
                order.append((first_near + n, True, None))
            else:
                f = n - NEAR_BLOCKS
                before = f < first_near
                order.append((jnp.where(before, f, f + NEAR_BLOCKS), False,
                              jnp.where(before, bias_before, bias_after)))
        return order

    def scores(u, i, j, near, cols):
        kb = k_ref[pl.ds(pl.multiple_of(j * tile, tile), tile), :]
        s = _dot(kb, qt_ref[u, :, cols])
        if near:
            bcol = cols.start % tile
            s = s + bias_ref[jnp.clip(j - i, -2, 2) + 2, :, bcol:bcol + QUERY_COLS]
        return s

    def softmax_update(u, s, j, const_bias, cols):
        if not stable:
            p = jnp.exp2(s)
            pv = jnp.concatenate([_dot(vt_ref[0, 0, j, :LANES, :], p.astype(_BF16)),
                                  jnp.broadcast_to(jnp.sum(p, axis=0, keepdims=True), (SUM_ROWS, QUERY_COLS))],
                                 axis=0)
            if const_bias is not None:
                pv = pv * jnp.exp2(const_bias)
            acc_ref[u, :, cols] += pv
            return
        m_ref = maybe_m_ref[0]
        m_old = m_ref[u, :, cols]
        col_max = jnp.max(s, axis=0, keepdims=True)
        if const_bias is not None:
            col_max = col_max + const_bias
        m_new = jnp.maximum(m_old, col_max)
        shift = m_new if const_bias is None else m_new - const_bias
        p = jnp.exp2(s - shift).astype(_BF16)
        m_ref[u, :, cols] = m_new
        acc_ref[u, :, cols] = jnp.exp2(m_old - m_new) * acc_ref[u, :, cols] + _dot(vt_ref[0, 0, j], p)

    def finish(u, lam):
        a = acc_ref[u, :LANES, :] / acc_ref[u, LANES:LANES + 1, :]
        ot = a[:, :tile] - lam * a[:, tile:]
        ms = jnp.mean(ot * ot, axis=0, keepdims=True)
        ot = ot * lax.rsqrt(ms + EPS) * (sg_ref[...] * (1.0 - lambda_init))
        o_ref[u * tile:(u + 1) * tile, :] = ot.T.astype(o_ref.dtype)

    streams = range(Q_PER_STEP)
    blocks = [Q_PER_STEP * step + u for u in streams]
    orders = [block_order(i) for i in blocks]
    for u in streams:
        prepare(u)
    ahead = [[scores(u, blocks[u], orders[u][0][0], orders[u][0][1], cols) for cols in col_slices]
             for u in streams]
    for n in range(n_blocks):
        for c, cols in enumerate(col_slices):
            for u in streams:
                j, _, const_bias = orders[u][n]
                current = ahead[u][c]
                if n + 1 < n_blocks:
                    ahead[u][c] = scores(u, blocks[u], orders[u][n + 1][0], orders[u][n + 1][1], cols)
                softmax_update(u, current, j, const_bias, cols)

    lp = lam_ref[...]
    lam = (jnp.exp(jnp.sum(lp[0:1] * lp[1:2], axis=-1, keepdims=True))
           - jnp.exp(jnp.sum(lp[2:3] * lp[3:4], axis=-1, keepdims=True)) + lambda_init)
    for u in streams:
        finish(u, lam)


def _diff_attn_call(q, k, vt, rel_bias, lam_p, subln_gain, *, batch, seq, heads, lambda_init, stable):
    n, d = q.shape
    tile = ATTN_TILE
    assert seq % tile == 0 and tile >= T5_MAX_DIST and seq >= 3 * tile and d == heads * LANES
    assert tile % QUERY_COLS == 0 and seq // tile >= NEAR_BLOCKS
    nq = seq // tile
    assert vt.shape == (batch, heads, nq, LANES + SUM_ROWS, tile) and nq % Q_PER_STEP == 0
    steps = nq // Q_PER_STEP
    q_spec = pl.BlockSpec((Q_PER_STEP * tile, LANES), lambda h, b, i: (b * steps + i, h))
    kern = functools.partial(_diff_attn_kernel, seq=seq, tile=tile, lambda_init=lambda_init, stable=stable)
    running_max = [pltpu.VMEM((Q_PER_STEP, 1, 2 * tile), _F32)] if stable else []
    return pl.pallas_call(
        kern,
        grid=(heads, batch, steps),
        in_specs=[q_spec,
                  pl.BlockSpec((seq, LANES), lambda h, b, i: (b, h)),
                  pl.BlockSpec((1, 1, nq, LANES + SUM_ROWS, tile), lambda h, b, i: (b, h, 0, 0, 0)),
                  pl.BlockSpec((1, 1, 2 * seq), lambda h, b, i: (h, 0, 0)),
                  pl.BlockSpec(lam_p.shape, lambda h, b, i: (0, 0)),
                  pl.BlockSpec((LANES, 1), lambda h, b, i: (0, 0))],
        out_specs=q_spec,
        out_shape=jax.ShapeDtypeStruct((n, d), _BF16),
        scratch_shapes=[pltpu.VMEM((5, tile, tile), _F32),
                        pltpu.VMEM((Q_PER_STEP, LANES, 2 * tile), _BF16),
                        pltpu.VMEM((Q_PER_STEP, LANES + SUM_ROWS, 2 * tile), _F32),
                        ] + running_max,
        compiler_params=pltpu.CompilerParams(dimension_semantics=("arbitrary",) * 3,
                                             vmem_limit_bytes=VMEM_LIMIT),
        name="diff_attn" if stable else "diff_attn_bounded",
    )(q, k, vt, rel_bias, lam_p, subln_gain.reshape(LANES, 1))


def _t5_bucket(rel):
    half = T5_BUCKETS // 2
    max_exact = half // 2
    ret = jnp.where(rel > 0, half, 0)
    n = jnp.abs(rel)
    nf = jnp.maximum(n, 1).astype(_F32)
    large = max_exact + (jnp.log(nf / max_exact) / math.log(T5_MAX_DIST / max_exact)
                         * (half - max_exact)).astype(jnp.int32)
    large = jnp.minimum(large, half - 1)
    return ret + jnp.where(n < max_exact, n, large)


NA_SLAB_ROWS = NA_ROWS + 2
NA_EDGE_STEPS = 2
INVALID = 2 * NA_ROWS - 1


def _na_row_start(r, rows):
    return min(max(r - NA_ROWS // 2, 0), rows - NA_ROWS)


def _natten_kernel(q_ref, k_ref, vt_ref, tab_ref, o_ref, blk_ref, bias_ref, *, rows, stable):
    w = GRID_W
    b = pl.program_id(1)
    lane = lax.broadcasted_iota(jnp.int32, (w, LANES), 1)
    low_lane = lane < w

    @pl.when(b == 0)
    def _():
        key_col = lax.broadcasted_iota(jnp.int32, (w, LANES), 0)
        col_start = jnp.clip((lane & (w - 1)) - NA_COLS // 2, 0, w - NA_COLS)
        in_window = (key_col >= col_start) & (key_col < col_start + NA_COLS)
        for e in range(2):
            for half in range(2):
                for dr in range(INVALID):
                    row = jnp.broadcast_to(tab_ref[e, half, dr:dr + 1, :], (w, LANES))
                    rolled = pltpu.roll(row, 0, 1, stride=1, stride_axis=0)
                    blk_ref[e, half, dr] = jnp.where(in_window, rolled, NEG_BIG)
                blk_ref[e, half, INVALID] = jnp.full((w, LANES), NEG_BIG, _F32)
        for variant, r0 in enumerate(_na_variant_rows(rows)):
            kstart = min(_na_row_start(r0, rows), rows - NA_SLAB_ROWS)
            for e in range(2):
                for dk in range(NA_SLAB_ROWS):
                    idx = []
                    for dq in range(2):
                        rs = _na_row_start(r0 + dq, rows)
                        inside = rs <= kstart + dk < rs + NA_ROWS
                        idx.append(kstart + dk - (r0 + dq) + NA_ROWS - 1 if inside else INVALID)
                    bias_ref[variant, dk * w:(dk + 1) * w, e * LANES:(e + 1) * LANES] = jnp.where(
                        low_lane, blk_ref[e, 0, idx[0]], blk_ref[e, 1, idx[1]])

    low_row = lax.broadcasted_iota(jnp.int32, (LANES, 1), 0) < HEAD_DIM
    n_steps = rows // 2

    def key_start(g):
        return min(_na_row_start(2 * g, rows), rows - NA_SLAB_ROWS)

    def query_rows(g):
        return slice(2 * g * w, 2 * (g + 1) * w)

    def scores(g):
        if g < NA_EDGE_STEPS:
            variant = g
        elif g >= n_steps - NA_EDGE_STEPS:
            variant = g - (n_steps - 2 * NA_EDGE_STEPS - 1)
        else:
            variant = NA_EDGE_STEPS
        qt = q_ref[query_rows(g), :].astype(_F32).T
        qst = jnp.concatenate([jnp.where(low_row, qt, 0.0), jnp.where(low_row, 0.0, qt)],
                              axis=1).astype(_BF16)
        kw = k_ref[key_start(g) * w:(key_start(g) + NA_SLAB_ROWS) * w, :]
        return _dot(kw, qst) + bias_ref[variant]

    def probs(s):
        if stable:
            return jnp.exp2(s - jnp.max(s, axis=0, keepdims=True)).astype(_BF16), None
        p = jnp.exp2(s)
        return p.astype(_BF16), jnp.sum(p, axis=0, keepdims=True)

    def output(g, p_and_sum):
        p, col_sum = p_and_sum
        vt = vt_ref[0, 0, key_start(g) // 2:(key_start(g) + NA_SLAB_ROWS) // 2]
        if stable:
            vt = jnp.concatenate([vt[t] for t in range(NA_SLAB_ROWS // 2)], axis=1)
            ot = _dot(vt, p)
            a = ot[:LANES] / ot[LANES:LANES + 1]
        else:
            vt = jnp.concatenate([vt[t, :LANES, :] for t in range(NA_SLAB_ROWS // 2)], axis=1)
            a = _dot(vt, p) / col_sum
        o_ref[query_rows(g), :] = jnp.where(low_row, a[:, :LANES], a[:, LANES:]).T.astype(o_ref.dtype)

    all_scores = [scores(g) for g in range(n_steps)]
    all_probs = [probs(s) for s in all_scores]
    for g, p in enumerate(all_probs):
        output(g, p)


def _na_variant_rows(rows):
    lead = [2 * g for g in range(NA_EDGE_STEPS)]
    trail = [rows - 2 * NA_EDGE_STEPS + 2 * g for g in range(NA_EDGE_STEPS)]
    return lead + [2 * NA_EDGE_STEPS] + trail


def _natten_call(q, k, vt, tab, *, batch, seq, stable):
    n, d = q.shape
    rows = seq // GRID_W
    pairs = d // LANES
    assert GRID_W * 2 == LANES and NA_ROWS // 2 <= 2 * NA_EDGE_STEPS and rows % 4 == 0
    assert rows >= NA_SLAB_ROWS + 4 * NA_EDGE_STEPS
    assert vt.shape == (batch, pairs, rows // 2, LANES + SUM_ROWS, LANES)
    spec = pl.BlockSpec((seq, LANES), lambda j, b: (b, j))
    n_variants = 2 * NA_EDGE_STEPS + 1
    return pl.pallas_call(
        functools.partial(_natten_kernel, rows=rows, stable=stable),
        grid=(pairs, batch),
        in_specs=[spec, spec,
                  pl.BlockSpec((1, 1) + vt.shape[2:], lambda j, b: (b, j, 0, 0, 0)),
                  pl.BlockSpec((2,) + tab.shape[1:], lambda j, b: (j, 0, 0, 0))],
        out_specs=spec,
        out_shape=jax.ShapeDtypeStruct((n, d), _BF16),
        scratch_shapes=[pltpu.VMEM((2, 2, INVALID + 1, GRID_W, LANES), _F32),
                        pltpu.VMEM((n_variants, NA_SLAB_ROWS * GRID_W, 2 * LANES), _F32)],
        compiler_params=pltpu.CompilerParams(dimension_semantics=("arbitrary",) * 2,
                                             vmem_limit_bytes=VMEM_LIMIT),
        name="natten" if stable else "natten_bounded",
    )(q, k, vt, tab)


def _natten_table(rpb):
    rev = rpb.astype(_F32)[:, :, ::-1] * LOG2E
    left = GRID_W - NA_COLS + 1
    plain = jnp.pad(rev, ((0, 0), (0, 0), (left, LANES - left - rev.shape[-1])))
    return jnp.stack([jnp.roll(plain, GRID_W, axis=-1), plain], axis=1)


def _scores_bounded(q_gain, k_gain, bias_table):
    f32 = lambda a: jnp.max(jnp.abs(a.astype(_F32)))
    bound = LOG2E * (1.02 * HEAD_DIM ** 0.5 * f32(q_gain) * f32(k_gain) + f32(bias_table))
    return bound <= SAFE_EXP2_RANGE


def kernel(x, p, norm_gains, w_ffn_in, w_ffn_out, t5_table, a_w_qkv, a_w_o, a_q_gain, a_k_gain, a_lambda,
           a_subln_gain, b_w_qkv, b_w_o, b_q_gain, b_k_gain, b_rpb, w_ple_gate, b_ple_gate, w_ple_proj):
    batch, seq, d = x.shape
    depth = p.shape[0]
    n = batch * seq
    assert a_q_gain.shape[1] == HEAD_DIM and b_q_gain.shape[1] == HEAD_DIM and seq % GRID_W == 0
    assert t5_table.shape[0] == T5_BUCKETS and b_rpb.shape[2:] == (2 * NA_ROWS - 1, 2 * NA_COLS - 1)
    bf = lambda w: w.astype(_BF16)
    lanes = lambda g, scale: jnp.tile(g.astype(_F32), d // g.shape[0]).reshape(1, d) * scale

    rel = seq - jnp.arange(2 * seq, dtype=jnp.int32)
    in_bucket = _t5_bucket(rel)[None, None, :] == jnp.arange(T5_BUCKETS, dtype=jnp.int32)[None, :, None]
    rel_bias = jnp.sum(jnp.where(in_bucket, t5_table.astype(_F32).T[:, :, None] * LOG2E, 0.0), axis=1)
    rel_bias = rel_bias.reshape(-1, 1, 2 * seq)

    w_gate, w_proj, wo_a, wo_b = bf(w_ple_gate), bf(w_ple_proj), bf(a_w_o), bf(b_w_o)
    w_in, w_out, w_qkv = bf(w_ffn_in[0, 0]), bf(w_ffn_out[0, 0]), bf(a_w_qkv[0])
    p_rows = p.reshape(depth * n, -1)
    q_scale = HEAD_DIM ** -0.5 * LOG2E

    h = x.reshape(n, d)
    for i in range(depth):
        j = i // 2
        q_gain, k_gain, keys = ((a_q_gain[j], a_k_gain[j], ATTN_TILE) if i % 2 == 0 else
                                (b_q_gain[j], b_k_gain[j], 2 * GRID_W))
        h, q, k, vt, w_in, w_out = _token_call(
            h, norm_gains[i, 0], w_in, w_out, (),
            qkv=(norm_gains[i, 1], w_qkv, (), lanes(q_gain, q_scale), lanes(k_gain, 1.0), batch, keys),
            cast=[(w_ffn_in, (i, 1)), (w_ffn_out, (i, 1))])
        if i % 2 == 0:
            lambda_init = 0.8 - 0.6 * math.exp(-0.3 * i)
            attend = functools.partial(_diff_attn_call, batch=batch, seq=seq, heads=d // LANES,
                                       lambda_init=lambda_init)
            bounded = _scores_bounded(a_q_gain[j], a_k_gain[j], t5_table)
            o = lax.cond(bounded, functools.partial(attend, stable=False), functools.partial(attend, stable=True),
                         q, k, vt, rel_bias, a_lambda[j], a_subln_gain[j])
            w_o = wo_a
        else:
            attend = functools.partial(_natten_call, batch=batch, seq=seq)
            bounded = _scores_bounded(b_q_gain[j], b_k_gain[j], b_rpb[j])
            o = lax.cond(bounded, functools.partial(attend, stable=False), functools.partial(attend, stable=True),
                         q, k, vt, _natten_table(b_rpb[j]))
            w_o = wo_b
        cast = []
        if i + 1 < depth:
            next_qkv = a_w_qkv if (i + 1) % 2 == 0 else b_w_qkv
            cast = [(w_ffn_in, (i + 1, 0)), (w_ffn_out, (i + 1, 0)), (next_qkv, ((i + 1) // 2,))]
        h, *cast_out = _token_call(h, norm_gains[i, 2], w_in, w_out, (), proj=(o, w_o, j),
                                   ple=(norm_gains[i, 3], w_gate, b_ple_gate[i], p_rows, w_proj, i), cast=cast)
        if cast_out:
            w_in, w_out, w_qkv = cast_out
    return h.reshape(batch, seq, d)
```

```python
import functools
import math

import jax
import jax.numpy as jnp
from jax import lax
from jax.experimental import pallas as pl
from jax.experimental.pallas import tpu as pltpu

EPS = 1e-6
LOG2E = math.log2(math.e)
NEG_BIG = -1e30
LANES = 128
HEAD_DIM = 64
VMEM_LIMIT = 56 * 1024 * 1024

T5_BUCKETS = 32
T5_MAX_DIST = 128
GRID_W = 64
NA_ROWS = 8
NA_COLS = 16

TOKEN_TILE = 512
FF_CHUNK = 256
ATTN_TILE = 512
QUERY_COLS = 256
Q_PER_STEP = 4
NEAR_BLOCKS = 3
SAFE_EXP2_RANGE = 60.0
SUM_ROWS = 16

_F32 = jnp.float32
_BF16 = jnp.bfloat16


def _rms(x, gain):
    return x * lax.rsqrt(jnp.mean(x * x, axis=-1, keepdims=True) + EPS) * gain


def _dot(a, b):
    return jnp.dot(a, b, preferred_element_type=_F32)


def _resident(arr, lead=()):
    tail = arr.shape[len(lead):]
    index = tuple(lead) + (0,) * len(tail)
    return pl.BlockSpec((None,) * len(lead) + tail, lambda *_: index, pipeline_mode=pl.Buffered(1))


def _head_rms_store(x, gain_ref, out_ref):
    low = lax.broadcasted_iota(jnp.int32, (1, LANES), 1) < HEAD_DIM
    for j in range(x.shape[1] // LANES):
        xb = x[:, j * LANES:(j + 1) * LANES]
        sq = xb * xb
        s_lo = jnp.sum(jnp.where(low, sq, 0.0), axis=-1, keepdims=True)
        s_hi = jnp.sum(jnp.where(low, 0.0, sq), axis=-1, keepdims=True)
        ms = jnp.where(low, s_lo, s_hi) * (1.0 / HEAD_DIM)
        yb = xb * lax.rsqrt(ms + EPS) * gain_ref[:, j * LANES:(j + 1) * LANES]
        out_ref[:, j * LANES:(j + 1) * LANES] = yb.astype(out_ref.dtype)


def _qkv_store(y, g_ref, w_ref, qg_ref, kg_ref, q_ref, k_ref, vt_ref):
    d = y.shape[1]
    xn = _rms(y, g_ref[...]).astype(_BF16)
    _head_rms_store(_dot(xn, w_ref[:, 0:d]), qg_ref, q_ref)
    _head_rms_store(_dot(xn, w_ref[:, d:2 * d]), kg_ref, k_ref)
    v = _dot(xn, w_ref[:, 2 * d:3 * d])
    groups, keys = vt_ref.shape[2], vt_ref.shape[4]
    ones = jnp.ones((SUM_ROWS, keys), vt_ref.dtype)
    for hb in range(d // LANES):
        vt = v[:, hb * LANES:(hb + 1) * LANES].T.astype(vt_ref.dtype)
        for t in range(groups):
            vt_ref[0, hb, t, :LANES, :] = vt[:, t * keys:(t + 1) * keys]
            vt_ref[0, hb, t, LANES:, :] = ones


def _token_kernel(*refs, has_proj, has_ple, has_qkv, n_casts, d_ff):
    it = iter(refs)
    h_ref = next(it)
    if has_proj:
        o_ref, wo_ref = next(it), next(it)
    g_ref, win_ref, wout_ref = next(it), next(it), next(it)
    if has_ple:
        g3_ref, wg_ref, bg_ref, p_ref, wp_ref = (next(it) for _ in range(5))
    if has_qkv:
        qkv_in = [next(it) for _ in range(4)]
    cast_src = [next(it) for _ in range(n_casts)]
    out_ref = next(it)

    x = h_ref[...]
    if has_proj:
        x = x + _dot(o_ref[...], wo_ref[...])
    xn = _rms(x, g_ref[...]).astype(_BF16)
    acc = jnp.zeros_like(x)
    for c in range(d_ff // FF_CHUNK):
        lo = c * FF_CHUNK
        g = _dot(xn, win_ref[:, lo:lo + FF_CHUNK])
        u = _dot(xn, win_ref[:, d_ff + lo:d_ff + lo + FF_CHUNK])
        a = (g * jax.nn.sigmoid(g) * u).astype(_BF16)
        acc = acc + _dot(a, wout_ref[lo:lo + FF_CHUNK, :])
    y = x + 0.5 * acc
    if has_ple:
        yn = _rms(y, g3_ref[...]).astype(_BF16)
        gate = jax.nn.sigmoid(_dot(yn, wg_ref[...]) + bg_ref[...])
        y = y + gate * _dot(p_ref[...].astype(_BF16), wp_ref[...])
    out_ref[...] = y
    if has_qkv:
        _qkv_store(y, *qkv_in, *(next(it) for _ in range(3)))
    for src_ref in cast_src:
        dst_ref = next(it)
        dst_ref[...] = src_ref[...].astype(dst_ref.dtype)


def _cast_chunks(w, lead, steps):
    rows, cols = w.shape[len(lead):]
    chunks = next(c for c in (steps, steps // 2, steps // 4) if rows % c == 0 and (rows // c) % 16 == 0)
    block = rows // chunks
    src = pl.BlockSpec((None,) * len(lead) + (block, cols),
                       lambda i: tuple(lead) + (jnp.minimum(i, chunks - 1), 0))
    dst = pl.BlockSpec((block, cols), lambda i: (jnp.minimum(i, chunks - 1), 0))
    return src, dst, jax.ShapeDtypeStruct((rows, cols), _BF16)


def _token_call(h, gain, w_in, w_out, lead, proj=None, ple=None, qkv=None, cast=()):
    n, d = h.shape
    d_ff = w_out.shape[-2]
    assert n % TOKEN_TILE == 0 and d_ff % FF_CHUNK == 0
    tiles = n // TOKEN_TILE
    row = lambda width, first=0: pl.BlockSpec((TOKEN_TILE, width), lambda i: (first + i, 0))
    vec = lambda v: v.reshape(1, d)
    args, specs = [h], [row(d)]
    if proj is not None:
        o, w_o, j = proj
        args += [o, w_o]
        specs += [row(o.shape[1]), _resident(w_o, (j,))]
    args += [vec(gain), w_in, w_out]
    specs += [_resident(vec(gain)), _resident(w_in, lead), _resident(w_out, lead)]
    if ple is not None:
        g3, w_g, b_g, p_rows, w_p, layer = ple
        args += [vec(g3), w_g, vec(b_g), p_rows, w_p]
        specs += [_resident(vec(g3)), _resident(w_g, (layer,)), _resident(vec(b_g)),
                  row(p_rows.shape[1], layer * tiles), _resident(w_p, (layer,))]
    out_specs, out_shape = [row(d)], [jax.ShapeDtypeStruct((n, d), _F32)]
    if qkv is not None:
        g1, w_qkv, qkv_lead, q_gain_lanes, k_gain_lanes, batch, keys = qkv
        seq = n // batch
        assert seq % TOKEN_TILE == 0 and TOKEN_TILE % keys == 0
        per_batch, groups = seq // TOKEN_TILE, TOKEN_TILE // keys
        args += [vec(g1), w_qkv, q_gain_lanes, k_gain_lanes]
        specs += [_resident(vec(g1)), _resident(w_qkv, qkv_lead), _resident(q_gain_lanes), _resident(k_gain_lanes)]
        out_specs += [row(d), row(d),
                      pl.BlockSpec((1, d // LANES, groups, LANES + SUM_ROWS, keys),
                                   lambda i: (i // per_batch, 0, i % per_batch, 0, 0))]
        out_shape += [jax.ShapeDtypeStruct((n, d), _BF16)] * 2
        out_shape += [jax.ShapeDtypeStruct((batch, d // LANES, seq // keys, LANES + SUM_ROWS, keys), _BF16)]
    for w, w_lead in cast:
        src_spec, dst_spec, dst_shape = _cast_chunks(w, w_lead, tiles)
        args.append(w)
        specs.append(src_spec)
        out_specs.append(dst_spec)
        out_shape.append(dst_shape)
    kern = functools.partial(_token_kernel, has_proj=proj is not None, has_ple=ple is not None,
                             has_qkv=qkv is not None, n_casts=len(cast), d_ff=d_ff)
    return pl.pallas_call(
        kern,
        grid=(tiles,),
        in_specs=specs,
        out_specs=out_specs,
        out_shape=out_shape,
        compiler_params=pltpu.CompilerParams(dimension_semantics=("arbitrary",),
                                             vmem_limit_bytes=VMEM_LIMIT),
        name="ffn_qkv" if qkv is not None else "proj_ffn_ple",
    )(*args)


def _diff_attn_kernel(q_ref, k_ref, vt_ref, rb_ref, lam_ref, sg_ref, o_ref,
                      bias_ref, qt_ref, acc_ref, *maybe_m_ref, seq, tile, lambda_init, stable):
    b, step = pl.program_id(1), pl.program_id(2)
    n_blocks = seq // tile

    @pl.when((b == 0) & (step == 0))
    def _():
        for t in range(-2, 3):
            w = rb_ref[0, :, seq - (t + 1) * tile:seq - (t - 1) * tile]
            rolled = pltpu.roll(jnp.broadcast_to(w, (tile, 2 * tile)), 0, 1, stride=1, stride_axis=0)
            bias_ref[t + 2] = rolled[:, tile:]

    low = lax.broadcasted_iota(jnp.int32, (LANES, 1), 0) < HEAD_DIM
    col_slices = [slice(c * QUERY_COLS, (c + 1) * QUERY_COLS) for c in range(2 * tile // QUERY_COLS)]
    bias_before = rb_ref[0, :, 2 * seq - 1:2 * seq]
    bias_after = rb_ref[0, :, 1:2]

    def prepare(u):
        qt = q_ref[u * tile:(u + 1) * tile, :].astype(_F32).T
        qt_ref[u, :, :tile] = jnp.where(low, qt, 0.0).astype(_BF16)
        qt_ref[u, :, tile:] = jnp.where(low, 0.0, qt).astype(_BF16)
        if stable:
            maybe_m_ref[0][u] = jnp.full(maybe_m_ref[0].shape[1:], NEG_BIG, _F32)
        acc_ref[u] = jnp.zeros(acc_ref.shape[1:], _F32)

    def block_order(i):
        first_near = jnp.clip(i - 1, 0, n_blocks - NEAR_BLOCKS)
        order = []
        for n in range(n_blocks):
            if n < NEAR_BLOCKS:
                order.append((first_near + n, True, None))
            else:
                f = n - NEAR_BLOCKS
                before = f < first_near
                order.append((jnp.where(before, f, f + NEAR_BLOCKS), False,
                              jnp.where(before, bias_before, bias_after)))
        return order

    def scores(u, i, j, near, cols):
        kb = k_ref[pl.ds(pl.multiple_of(j * tile, tile), tile), :]
        s = _dot(kb, qt_ref[u, :, cols])
        if near:
            bcol = cols.start % tile
            s = s + bias_ref[jnp.clip(j - i, -2, 2) + 2, :, bcol:bcol + QUERY_COLS]
        return s

    def softmax_update(u, s, j, const_bias, cols):
        if not stable:
            p = jnp.exp2(s)
            pv = jnp.concatenate([_dot(vt_ref[0, 0, j, :LANES, :], p.astype(_BF16)),
                                  jnp.broadcast_to(jnp.sum(p, axis=0, keepdims=True), (SUM_ROWS, QUERY_COLS))],
                                 axis=0)
            if const_bias is not None:
                pv = pv * jnp.exp2(const_bias)
            acc_ref[u, :, cols] += pv
            return
        m_ref = maybe_m_ref[0]
        m_old = m_ref[u, :, cols]
        col_max = jnp.max(s, axis=0, keepdims=True)
        if const_bias is not None:
            col_max = col_max + const_bias
        m_new = jnp.maximum(m_old, col_max)
        shift = m_new if const_bias is None else m_new - const_bias
        p = jnp.exp2(s - shift).astype(_BF16)
        m_ref[u, :, cols] = m_new
        acc_ref[u, :, cols] = jnp.exp2(m_old - m_new) * acc_ref[u, :, cols] + _dot(vt_ref[0, 0, j], p)

    def finish(u, lam):
        a = acc_ref[u, :LANES, :] / acc_ref[u, LANES:LANES + 1, :]
        ot = a[:, :tile] - lam * a[:, tile:]
        ms = jnp.mean(ot * ot, axis=0, keepdims=True)
        ot = ot * lax.rsqrt(ms + EPS) * (sg_ref[...] * (1.0 - lambda_init))
        o_ref[u * tile:(u + 1) * tile, :] = ot.T.astype(o_ref.dtype)

    streams = range(Q_PER_STEP)
    blocks = [Q_PER_STEP * step + u for u in streams]
    orders = [block_order(i) for i in blocks]
    for u in streams:
        prepare(u)
    ahead = [[scores(u, blocks[u], orders[u][0][0], orders[u][0][1], cols) for cols in col_slices]
             for u in streams]
    for n in range(n_blocks):
        for c, cols in enumerate(col_slices):
            for u in streams:
                j, _, const_bias = orders[u][n]
                current = ahead[u][c]
                if n + 1 < n_blocks:
                    ahead[u][c] = scores(u, blocks[u], orders[u][n + 1][0], orders[u][n + 1][1], cols)
                softmax_update(u, current, j, const_bias, cols)

    lp = lam_ref[...]
    lam = (jnp.exp(jnp.sum(lp[0:1] * lp[1:2], axis=-1, keepdims=True))
           - jnp.exp(jnp.sum(lp[2:3] * lp[3:4], axis=-1, keepdims=True)) + lambda_init)
    for u in streams:
        finish(u, lam)


def _diff_attn_call(q, k, vt, rel_bias, lam_p, subln_gain, *, batch, seq, heads, lambda_init, stable):
    n, d = q.shape
    tile = ATTN_TILE
    assert seq % tile == 0 and tile >= T5_MAX_DIST and seq >= 3 * tile and d == heads * LANES
    assert tile % QUERY_COLS == 0 and seq // tile >= NEAR_BLOCKS
    nq = seq // tile
    assert vt.shape == (batch, heads, nq, LANES + SUM_ROWS, tile) and nq % Q_PER_STEP == 0
    steps = nq // Q_PER_STEP
    q_spec = pl.BlockSpec((Q_PER_STEP * tile, LANES), lambda h, b, i: (b * steps + i, h))
    kern = functools.partial(_diff_attn_kernel, seq=seq, tile=tile, lambda_init=lambda_init, stable=stable)
    running_max = [pltpu.VMEM((Q_PER_STEP, 1, 2 * tile), _F32)] if stable else []
    return pl.pallas_call(
        kern,
        grid=(heads, batch, steps),
        in_specs=[q_spec,
                  pl.BlockSpec((seq, LANES), lambda h, b, i: (b, h)),
                  pl.BlockSpec((1, 1, nq, LANES + SUM_ROWS, tile), lambda h, b, i: (b, h, 0, 0, 0)),
                  pl.BlockSpec((1, 1, 2 * seq), lambda h, b, i: (h, 0, 0)),
                  pl.BlockSpec(lam_p.shape, lambda h, b, i: (0, 0)),
                  pl.BlockSpec((LANES, 1), lambda h, b, i: (0, 0))],
        out_specs=q_spec,
        out_shape=jax.ShapeDtypeStruct((n, d), _BF16),
        scratch_shapes=[pltpu.VMEM((5, tile, tile), _F32),
                        pltpu.VMEM((Q_PER_STEP, LANES, 2 * tile), _BF16),
                        pltpu.VMEM((Q_PER_STEP, LANES + SUM_ROWS, 2 * tile), _F32),
                        ] + running_max,
        compiler_params=pltpu.CompilerParams(dimension_semantics=("arbitrary",) * 3,
                                             vmem_limit_bytes=VMEM_LIMIT),
        name="diff_attn" if stable else "diff_attn_bounded",
    )(q, k, vt, rel_bias, lam_p, subln_gain.reshape(LANES, 1))


def _t5_bucket(rel):
    half = T5_BUCKETS // 2
    max_exact = half // 2
    ret = jnp.where(rel > 0, half, 0)
    n = jnp.abs(rel)
    nf = jnp.maximum(n, 1).astype(_F32)
    large = max_exact + (jnp.log(nf / max_exact) / math.log(T5_MAX_DIST / max_exact)
                         * (half - max_exact)).astype(jnp.int32)
    large = jnp.minimum(large, half - 1)
    return ret + jnp.where(n < max_exact, n, large)


NA_SLAB_ROWS = NA_ROWS + 2
NA_EDGE_STEPS = 2
INVALID = 2 * NA_ROWS - 1


def _na_row_start(r, rows):
    return min(max(r - NA_ROWS // 2, 0), rows - NA_ROWS)


def _natten_kernel(q_ref, k_ref, vt_ref, tab_ref, o_ref, blk_ref, bias_ref, *, rows, stable):
    w = GRID_W
    b = pl.program_id(1)
    lane = lax.broadcasted_iota(jnp.int32, (w, LANES), 1)
    low_lane = lane < w

    @pl.when(b == 0)
    def _():
        key_col = lax.broadcasted_iota(jnp.int32, (w, LANES), 0)
        col_start = jnp.clip((lane & (w - 1)) - NA_COLS // 2, 0, w - NA_COLS)
        in_window = (key_col >= col_start) & (key_col < col_start + NA_COLS)
        for e in range(2):
            for half in range(2):
                for dr in range(INVALID):
                    row = jnp.broadcast_to(tab_ref[e, half, dr:dr + 1, :], (w, LANES))
                    rolled = pltpu.roll(row, 0, 1, stride=1, stride_axis=0)
                    blk_ref[e, half, dr] = jnp.where(in_window, rolled, NEG_BIG)
                blk_ref[e, half, INVALID] = jnp.full((w, LANES), NEG_BIG, _F32)
        for variant, r0 in enumerate(_na_variant_rows(rows)):
            kstart = min(_na_row_start(r0, rows), rows - NA_SLAB_ROWS)
            for e in range(2):
                for dk in range(NA_SLAB_ROWS):
                    idx = []
                    for dq in range(2):
                        rs = _na_row_start(r0 + dq, rows)
                        inside = rs <= kstart + dk < rs + NA_ROWS
                        idx.append(kstart + dk - (r0 + dq) + NA_ROWS - 1 if inside else INVALID)
                    bias_ref[variant, dk * w:(dk + 1) * w, e * LANES:(e + 1) * LANES] = jnp.where(
                        low_lane, blk_ref[e, 0, idx[0]], blk_ref[e, 1, idx[1]])

    low_row = lax.broadcasted_iota(jnp.int32, (LANES, 1), 0) < HEAD_DIM
    n_steps = rows // 2

    def key_start(g):
        return min(_na_row_start(2 * g, rows), rows - NA_SLAB_ROWS)

    def query_rows(g):
        return slice(2 * g * w, 2 * (g + 1) * w)

    def scores(g):
        if g < NA_EDGE_STEPS:
            variant = g
        elif g >= n_steps - NA_EDGE_STEPS:
            variant = g - (n_steps - 2 * NA_EDGE_STEPS - 1)
        else:
            variant = NA_EDGE_STEPS
        qt = q_ref[query_rows(g), :].astype(_F32).T
        qst = jnp.concatenate([jnp.where(low_row, qt, 0.0), jnp.where(low_row, 0.0, qt)],
                              axis=1).astype(_BF16)
        kw = k_ref[key_start(g) * w:(key_start(g) + NA_SLAB_ROWS) * w, :]
        return _dot(kw, qst) + bias_ref[variant]

    def probs(s):
        if stable:
            return jnp.exp2(s - jnp.max(s, axis=0, keepdims=True)).astype(_BF16), None
        p = jnp.exp2(s)
        return p.astype(_BF16), jnp.sum(p, axis=0, keepdims=True)

    def output(g, p_and_sum):
        p, col_sum = p_and_sum
        vt = vt_ref[0, 0, key_start(g) // 2:(key_start(g) + NA_SLAB_ROWS) // 2]
        if stable:
            vt = jnp.concatenate([vt[t] for t in range(NA_SLAB_ROWS // 2)], axis=1)
            ot = _dot(vt, p)
            a = ot[:LANES] / ot[LANES:LANES + 1]
        else:
            vt = jnp.concatenate([vt[t, :LANES, :] for t in range(NA_SLAB_ROWS // 2)], axis=1)
            a = _dot(vt, p) / col_sum
        o_ref[query_rows(g), :] = jnp.where(low_row, a[:, :LANES], a[:, LANES:]).T.astype(o_ref.dtype)

    all_scores = [scores(g) for g in range(n_steps)]
    all_probs = [probs(s) for s in all_scores]
    for g, p in enumerate(all_probs):
        output(g, p)


def _na_variant_rows(rows):
    lead = [2 * g for g in range(NA_EDGE_STEPS)]
    trail = [rows - 2 * NA_EDGE_STEPS + 2 * g for g in range(NA_EDGE_STEPS)]
    return lead + [2 * NA_EDGE_STEPS] + trail


def _natten_call(q, k, vt, tab, *, batch, seq, stable):
    n, d = q.shape
    rows = seq // GRID_W
    pairs = d // LANES
    assert GRID_W * 2 == LANES and NA_ROWS // 2 <= 2 * NA_EDGE_STEPS and rows % 4 == 0
    assert rows >= NA_SLAB_ROWS + 4 * NA_EDGE_STEPS
    assert vt.shape == (batch, pairs, rows // 2, LANES + SUM_ROWS, LANES)
    spec = pl.BlockSpec((seq, LANES), lambda j, b: (b, j))
    n_variants = 2 * NA_EDGE_STEPS + 1
    return pl.pallas_call(
        functools.partial(_natten_kernel, rows=rows, stable=stable),
        grid=(pairs, batch),
        in_specs=[spec, spec,
                  pl.BlockSpec((1, 1) + vt.shape[2:], lambda j, b: (b, j, 0, 0, 0)),
                  pl.BlockSpec((2,) + tab.shape[1:], lambda j, b: (j, 0, 0, 0))],
        out_specs=spec,
        out_shape=jax.ShapeDtypeStruct((n, d), _BF16),
        scratch_shapes=[pltpu.VMEM((2, 2, INVALID + 1, GRID_W, LANES), _F32),
                        pltpu.VMEM((n_variants, NA_SLAB_ROWS * GRID_W, 2 * LANES), _F32)],
        compiler_params=pltpu.CompilerParams(dimension_semantics=("arbitrary",) * 2,
                                             vmem_limit_bytes=VMEM_LIMIT),
        name="natten" if stable else "natten_bounded",
    )(q, k, vt, tab)


def _natten_table(rpb):
    rev = rpb.astype(_F32)[:, :, ::-1] * LOG2E
    left = GRID_W - NA_COLS + 1
    plain = jnp.pad(rev, ((0, 0), (0, 0), (left, LANES - left - rev.shape[-1])))
    return jnp.stack([jnp.roll(plain, GRID_W, axis=-1), plain], axis=1)


def _scores_bounded(q_gain, k_gain, bias_table):
    f32 = lambda a: jnp.max(jnp.abs(a.astype(_F32)))
    bound = LOG2E * (1.02 * HEAD_DIM ** 0.5 * f32(q_gain) * f32(k_gain) + f32(bias_table))
    return bound <= SAFE_EXP2_RANGE


def kernel(x, p, norm_gains, w_ffn_in, w_ffn_out, t5_table, a_w_qkv, a_w_o, a_q_gain, a_k_gain, a_lambda,
           a_subln_gain, b_w_qkv, b_w_o, b_q_gain, b_k_gain, b_rpb, w_ple_gate, b_ple_gate, w_ple_proj):
    batch, seq, d = x.shape
    depth = p.shape[0]
    n = batch * seq
    assert a_q_gain.shape[1] == HEAD_DIM and b_q_gain.shape[1] == HEAD_DIM and seq % GRID_W == 0
    assert t5_table.shape[0] == T5_BUCKETS and b_rpb.shape[2:] == (2 * NA_ROWS - 1, 2 * NA_COLS - 1)
    bf = lambda w: w.astype(_BF16)
    lanes = lambda g, scale: jnp.tile(g.astype(_F32), d // g.shape[0]).reshape(1, d) * scale

    near = jnp.arange(-T5_MAX_DIST, T5_MAX_DIST + 1, dtype=jnp.int32)
    near_bias = t5_table.astype(_F32)[_t5_bucket(near)] * LOG2E
    after = jnp.broadcast_to(near_bias[-1:], (seq - T5_MAX_DIST, near_bias.shape[1]))
    before = jnp.broadcast_to(near_bias[:1], (seq - T5_MAX_DIST - 1, near_bias.shape[1]))
    rel_bias = jnp.concatenate([after, near_bias[::-1], before], axis=0).T.reshape(-1, 1, 2 * seq)

    w_gate, w_proj, wo_a, wo_b = bf(w_ple_gate), bf(w_ple_proj), bf(a_w_o), bf(b_w_o)
    w_in, w_out, w_qkv = bf(w_ffn_in[0, 0]), bf(w_ffn_out[0, 0]), bf(a_w_qkv[0])
    p_rows = p.reshape(depth * n, -1)
    q_scale = HEAD_DIM ** -0.5 * LOG2E

    h = x.reshape(n, d)
    for i in range(depth):
        j = i // 2
        q_gain, k_gain, keys = ((a_q_gain[j], a_k_gain[j], ATTN_TILE) if i % 2 == 0 else
                                (b_q_gain[j], b_k_gain[j], 2 * GRID_W))
        h, q, k, vt, w_in, w_out = _token_call(
            h, norm_gains[i, 0], w_in, w_out, (),
            qkv=(norm_gains[i, 1], w_qkv, (), lanes(q_gain, q_scale), lanes(k_gain, 1.0), batch, keys),
            cast=[(w_ffn_in, (i, 1)), (w_ffn_out, (i, 1))])
        if i % 2 == 0:
            lambda_init = 0.8 - 0.6 * math.exp(-0.3 * i)
            attend = functools.partial(_diff_attn_call, batch=batch, seq=seq, heads=d // LANES,
                                       lambda_init=lambda_init)
            bounded = _scores_bounded(a_q_gain[j], a_k_gain[j], t5_table)
            o = lax.cond(bounded, functools.partial(attend, stable=False), functools.partial(attend, stable=True),
                         q, k, vt, rel_bias, a_lambda[j], a_subln_gain[j])
            w_o = wo_a
        else:
            attend = functools.partial(_natten_call, batch=batch, seq=seq)
            bounded = _scores_bounded(b_q_gain[j], b_k_gain[j], b_rpb[j])
            o = lax.cond(bounded, functools.partial(attend, stable=False), functools.partial(attend, stable=True),
                         q, k, vt, _natten_table(b_rpb[j]))
            w_o = wo_b
        cast = []
        if i + 1 < depth:
            next_qkv = a_w_qkv if (i + 1) % 2 == 0 else b_w_qkv
            cast = [(w_ffn_in, (i + 1, 0)), (w_ffn_out, (i + 1, 0)), (next_qkv, ((i + 1) // 2,))]
        h, *cast_out = _token_call(h, norm_gains[i, 2], w_in, w_out, (), proj=(o, w_o, j),
                                   ple=(norm_gains[i, 3], w_gate, b_ple_gate[i], p_rows, w_proj, i), cast=cast)
        if cast_out:
            w_in, w_out, w_qkv = cast_out
    return h.reshape(batch, seq, d)
```
